```python
import math
import jax, jax.numpy as jnp
from jax import lax
import numpy as np


D_MODEL = 4096
BATCH = 1
SEQ = 8192
DEPTH = 2

N_A = max(1, DEPTH // 2)
N_B = DEPTH - N_A
HG_EXPAND = 128
HG_HEADS = D_MODEL // HG_EXPAND
HG_FDIM = HG_HEADS * HG_EXPAND
HG_HEAD_V = D_MODEL // HG_HEADS
HG_CHUNK = 64
HG_IN_DIM = 2 * HG_FDIM + 2 * D_MODEL
MLA_HEADS = D_MODEL // 64
Q_LORA = D_MODEL // 4
KV_LORA = 512
NOPE_DIM = 128
ROPE_DIM = 64
V_HEAD_DIM = 128
QK_HEAD_DIM = NOPE_DIM + ROPE_DIM
ROPE_THETA = 10000.0
Q_BLOCK = 128
ATTN_SCALE = 1.0 / math.sqrt(QK_HEAD_DIM)
FFN_HIDDEN = -(-8 * D_MODEL // (3 * 256)) * 256
NORM_EPS = 1e-6
POS_OFFSET_MAX = 4096

kernel_name = 'yoco_hgrn2_mla_hybrid'


def rms_norm(x, g):
    xf = x.astype(jnp.float32)
    y = xf * lax.rsqrt(jnp.mean(xf * xf, axis=-1, keepdims=True) + NORM_EPS)
    return (y * g.astype(jnp.float32)).astype(x.dtype)


def modulate(h, shift, scale):
    return h * (1 + scale[:, None, :]) + shift[:, None, :]


def swiglu(h, w_gate, w_up, w_down):
    return (jax.nn.silu(h @ w_gate) * (h @ w_up)) @ w_down


def rope_tables(positions):
    inv_freq = 1.0 / (ROPE_THETA ** (jnp.arange(0, ROPE_DIM, 2, dtype=jnp.float32) / ROPE_DIM))
    ang = positions.astype(jnp.float32)[..., None] * inv_freq
    return jnp.cos(ang)[:, :, None, :], jnp.sin(ang)[:, :, None, :]


def apply_rope(t, cos, sin):
    t1, t2 = jnp.split(t, 2, axis=-1)
    cos = cos.astype(t.dtype)
    sin = sin.astype(t.dtype)
    return jnp.concatenate([t1 * cos - t2 * sin, t2 * cos + t1 * sin], axis=-1)


def chunk_gated_recurrence(q, k, v, log_f):
    B, S, H, _ = q.shape
    Dv = v.shape[-1]
    n_chunks = S // HG_CHUNK

    def to_chunks(t):
        return t.astype(jnp.float32).reshape(B, n_chunks, HG_CHUNK, H, -1).transpose(1, 0, 3, 2, 4)

    qc, kc, vc = to_chunks(q), to_chunks(k), to_chunks(v)
    bc = jnp.cumsum(to_chunks(log_f), axis=3)
    causal = jnp.tril(jnp.ones((HG_CHUNK, HG_CHUNK), dtype=bool))[:, :, None]

    def step(state, inp):
        q_, k_, v_, b_ = inp
        rel = b_[:, :, :, None, :] - b_[:, :, None, :, :]
        decay = jnp.exp(jnp.where(causal, rel, -jnp.inf))
        attn = jnp.einsum('bhtd,bhsd,bhtsd->bhts', q_, k_, decay)
        o = (jnp.einsum('bhts,bhsv->bhtv', attn, v_)
             + jnp.einsum('bhtd,bhdv->bhtv', q_ * jnp.exp(b_), state))
        b_last = b_[:, :, -1:, :]
        state = (jnp.exp(b_last)[:, :, 0, :, None] * state
                 + jnp.einsum('bhsd,bhsv->bhdv', k_ * jnp.exp(b_last - b_), v_))
        return state, o

    state0 = jnp.zeros((B, H, q.shape[-1], Dv), jnp.float32)
    _, o = lax.scan(step, state0, (qc, kc, vc, bc))
    return o.transpose(1, 0, 3, 2, 4).reshape(B, S, H, Dv).astype(q.dtype)


def hgrn2_mixer(h, w_in, lb, norm_g, w_out):
    B, S, _ = h.shape
    proj = h @ w_in
    q, f_logit, i, g = jnp.split(proj, [HG_FDIM, 2 * HG_FDIM, 2 * HG_FDIM + D_MODEL], axis=-1)
    q = jax.nn.silu(q)
    f = lb + (1.0 - lb) * jax.nn.sigmoid(f_logit.astype(jnp.float32))
    log_f = jnp.log(f)
    k = (1.0 - f).astype(h.dtype)
    shp = (B, S, HG_HEADS, -1)
    o = chunk_gated_recurrence(q.reshape(shp), k.reshape(shp), i.reshape(shp), log_f.reshape(shp))
    o = rms_norm(o, norm_g) * jax.nn.silu(g.reshape(shp))
    return o.reshape(B, S, D_MODEL) @ w_out


def shared_mla_kv(x, kv_norm_in_g, w_dkv, kv_norm_g, w_ukv, cos, sin):
    B, S, _ = x.shape
    src = rms_norm(x, kv_norm_in_g)
    ckv = src @ w_dkv
    c_kv, k_pe = jnp.split(ckv, [KV_LORA], axis=-1)
    c_kv = rms_norm(c_kv, kv_norm_g)
    kv = (c_kv @ w_ukv).reshape(B, S, MLA_HEADS, NOPE_DIM + V_HEAD_DIM)
    k_nope, v = jnp.split(kv, [NOPE_DIM], axis=-1)
    k_pe = apply_rope(k_pe[:, :, None, :], cos, sin)
    k = jnp.concatenate([k_nope, jnp.broadcast_to(k_pe, (B, S, MLA_HEADS, ROPE_DIM))], axis=-1)
    return k, v


def causal_block_attention(q, k, v):
    B, S, H, Dqk = q.shape
    n_blocks = S // Q_BLOCK
    qb = q.reshape(B, n_blocks, Q_BLOCK, H, Dqk).transpose(1, 0, 2, 3, 4)
    starts = jnp.arange(n_blocks, dtype=jnp.int32) * Q_BLOCK
    kpos = jnp.arange(S, dtype=jnp.int32)
    qoff = jnp.arange(Q_BLOCK, dtype=jnp.int32)

    def one_block(args):
        qblk, start = args
        s = jnp.einsum('bqhd,bkhd->bhqk', qblk, k).astype(jnp.float32) * ATTN_SCALE
        mask = (start + qoff)[:, None] >= kpos[None, :]
        p = jax.nn.softmax(jnp.where(mask, s, -jnp.inf), axis=-1)
        return jnp.einsum('bhqk,bkhv->bqhv', p.astype(v.dtype), v)

    o = lax.map(one_block, (qb, starts))
    return o.transpose(1, 0, 2, 3, 4).reshape(B, S, H, v.shape[-1])


def mla_mixer(h, w_dq, q_norm_g, w_uq, w_o, k, v, cos, sin):
    B, S, _ = h.shape
    q = (rms_norm(h @ w_dq, q_norm_g) @ w_uq).reshape(B, S, MLA_HEADS, QK_HEAD_DIM)
    q_nope, q_pe = jnp.split(q, [NOPE_DIM], axis=-1)
    q = jnp.concatenate([q_nope, apply_rope(q_pe, cos, sin)], axis=-1)
    o = causal_block_attention(q, k, v)
    return o.reshape(B, S, MLA_HEADS * V_HEAD_DIM) @ w_o


def _normal(key, shape, scale):
    return jax.random.normal(key, shape, jnp.float32) * scale


def setup_inputs(seed: int = 0) -> dict:
    key = jax.random.key(seed)
    ks = jax.random.split(key, 21)
    D = D_MODEL
    x = _normal(ks[0], (BATCH, SEQ, D), 1.0)
    c = _normal(ks[1], (BATCH, D), 1.0)
    positions = (jax.random.randint(ks[2], (BATCH, 1), 0, POS_OFFSET_MAX, jnp.int32)
                 + jnp.arange(SEQ, dtype=jnp.int32)[None, :])
    ada_w = _normal(ks[3], (DEPTH, D, 6 * D), D ** -0.5)
    ada_b = _normal(ks[4], (DEPTH, 6 * D), 0.02)
    norm_g = 1.0 + _normal(ks[5], (DEPTH, 4, D), 0.05)
    ffn_w_gate = _normal(ks[6], (DEPTH, D, FFN_HIDDEN), D ** -0.5)
    ffn_w_up = _normal(ks[7], (DEPTH, D, FFN_HIDDEN), D ** -0.5)
    ffn_w_down = _normal(ks[8], (DEPTH, FFN_HIDDEN, D), FFN_HIDDEN ** -0.5)
    hg_w_in = _normal(ks[9], (N_A, D, HG_IN_DIM), D ** -0.5)
    hg_lb_logits = _normal(ks[10], (N_A + 1, HG_FDIM), 0.5)
    hg_norm_g = 1.0 + _normal(ks[11], (N_A, HG_HEAD_V), 0.05)
    hg_w_out = _normal(ks[12], (N_A, D, D), D ** -0.5)
    mla_w_dq = _normal(ks[13], (N_B, D, Q_LORA), D ** -0.5)
    mla_q_norm_g = 1.0 + _normal(ks[14], (N_B, Q_LORA), 0.05)
    mla_w_uq = _normal(ks[15], (N_B, Q_LORA, MLA_HEADS * QK_HEAD_DIM), Q_LORA ** -0.5)
    mla_w_o = _normal(ks[16], (N_B, MLA_HEADS * V_HEAD_DIM, D), (MLA_HEADS * V_HEAD_DIM) ** -0.5)
    kv_norm_in_g = 1.0 + _normal(ks[17], (D,), 0.05)
    kv_w_dkv = _normal(ks[18], (D, KV_LORA + ROPE_DIM), D ** -0.5)
    kv_norm_g = 1.0 + _normal(ks[19], (KV_LORA,), 0.05)
    kv_w_ukv = _normal(ks[20], (KV_LORA, MLA_HEADS * (NOPE_DIM + V_HEAD_DIM)), KV_LORA ** -0.5)
    return {'x': x, 'c': c, 'positions': positions,
            'ada_w': ada_w, 'ada_b': ada_b, 'norm_g': norm_g,
            'ffn_w_gate': ffn_w_gate, 'ffn_w_up': ffn_w_up, 'ffn_w_down': ffn_w_down,
            'hg_w_in': hg_w_in, 'hg_lb_logits': hg_lb_logits, 'hg_norm_g': hg_norm_g, 'hg_w_out': hg_w_out,
            'mla_w_dq': mla_w_dq, 'mla_q_norm_g': mla_q_norm_g, 'mla_w_uq': mla_w_uq, 'mla_w_o': mla_w_o,
            'kv_norm_in_g': kv_norm_in_g, 'kv_w_dkv': kv_w_dkv, 'kv_norm_g': kv_norm_g, 'kv_w_ukv': kv_w_ukv}


def reference(x, c, positions, ada_w, ada_b, norm_g, ffn_w_gate, ffn_w_up, ffn_w_down,
              hg_w_in, hg_lb_logits, hg_norm_g, hg_w_out,
              mla_w_dq, mla_q_norm_g, mla_w_uq, mla_w_o,
              kv_norm_in_g, kv_w_dkv, kv_norm_g, kv_w_ukv):
    cos, sin = rope_tables(positions)
    lower_bounds = jnp.cumsum(jax.nn.softmax(hg_lb_logits.astype(jnp.float32), axis=0), axis=0)
    c_act = jax.nn.silu(c)
    k_shared = None
    v_shared = None
    for l in range(DEPTH):
        mod = c_act @ ada_w[l] + ada_b[l]
        sh_m, sc_m, g_m, sh_f, sc_f, g_f = jnp.split(mod, 6, axis=-1)
        h = modulate(rms_norm(x, norm_g[l, 0]), sh_m, sc_m)
        if l < N_A:
            y = hgrn2_mixer(h, hg_w_in[l], lower_bounds[l], hg_norm_g[l], hg_w_out[l])
        else:
            j = l - N_A
            y = mla_mixer(h, mla_w_dq[j], mla_q_norm_g[j], mla_w_uq[j], mla_w_o[j],
                          k_shared, v_shared, cos, sin)
        x = x + g_m[:, None, :] * rms_norm(y, norm_g[l, 1])
        h = modulate(rms_norm(x, norm_g[l, 2]), sh_f, sc_f)
        y = swiglu(h, ffn_w_gate[l], ffn_w_up[l], ffn_w_down[l])
        x = x + g_f[:, None, :] * rms_norm(y, norm_g[l, 3])
        if l == N_A - 1:
            k_shared, v_shared = shared_mla_kv(x, kv_norm_in_g, kv_w_dkv, kv_norm_g, kv_w_ukv, cos, sin)
    return x
```

```python
import functools
import math

import jax
import jax.numpy as jnp
from jax import lax
from jax.experimental import pallas as pl
from jax.experimental.pallas import tpu as pltpu

F32 = jnp.float32
BF16 = jnp.bfloat16

HG_HEAD_DIM = 128
NOPE_DIM = 128
ROPE_DIM = 64
V_HEAD_DIM = 128
QK_HEAD_DIM = NOPE_DIM + ROPE_DIM
KV_LORA = 512
ROPE_THETA = 10000.0
NORM_EPS = 1e-6
ATTN_SCALE = 1.0 / math.sqrt(QK_HEAD_DIM)

LANES = 128
SUBLANES = 8
VMEM_BYTES_V7X = 64 * 1024 * 1024
VMEM_LIMIT_CAP = VMEM_BYTES_V7X - 8 * 1024 * 1024
VMEM_LIMIT_FLOOR = 32 * 1024 * 1024

Q_HEAD_PAD = 2 * LANES
HG_CHUNK = 128
HG_LEVELS = 7


def _vmem_limit(block_bytes, temp_bytes=0):
    est = 2 * sum(block_bytes) + temp_bytes + (4 << 20)
    return int(min(max(est, VMEM_LIMIT_FLOOR), VMEM_LIMIT_CAP))


def _nbytes(shape, dtype):
    return math.prod(shape) * jnp.dtype(dtype).itemsize


def _tile(dim, pref):
    if dim <= pref:
        return dim
    t = (pref // LANES) * LANES
    while t >= LANES:
        if dim % t == 0:
            return t
        t -= LANES
    return dim


def _sigmoid(x):
    return 1.0 / (1.0 + jnp.exp(-x))


def _silu(x):
    return x * _sigmoid(x)


def _rms(x, g):
    ms = jnp.mean(x * x, axis=-1, keepdims=True)
    return x * lax.rsqrt(ms + NORM_EPS) * g


def _ada_kernel(c_ref, w_ref, b_ref, o_ref, acc_ref, *, nk):
    k = pl.program_id(2)
    tk, tn = w_ref.shape[1], w_ref.shape[2]
    p = w_ref[0] * _silu(c_ref[...])
    part = p.reshape(tk // SUBLANES, SUBLANES, tn).sum(axis=0)

    @pl.when(k == 0)
    def _():
        acc_ref[...] = part

    @pl.when(k > 0)
    def _():
        acc_ref[...] += part

    @pl.when(k == nk - 1)
    def _():
        o_ref[0] = acc_ref[...].sum(axis=0, keepdims=True) + b_ref[0]


def _ada_call(c, ada_w, ada_b):
    L, D, N = ada_w.shape
    tk, tn = _tile(D, 1024), _tile(N, 2048)
    nk = D // tk
    c_col = c.reshape(D, 1)
    b3 = ada_b.reshape(L, 1, N)
    return pl.pallas_call(
        functools.partial(_ada_kernel, nk=nk),
        grid=(L, N // tn, nk),
        in_specs=[
            pl.BlockSpec((tk, 1), lambda l, j, k: (k, 0)),
            pl.BlockSpec((1, tk, tn), lambda l, j, k: (l, k, j)),
            pl.BlockSpec((1, 1, tn), lambda l, j, k: (l, 0, j)),
        ],
        out_specs=pl.BlockSpec((1, 1, tn), lambda l, j, k: (l, 0, j)),
        out_shape=jax.ShapeDtypeStruct((L, 1, N), F32),
        scratch_shapes=[pltpu.VMEM((SUBLANES, tn), F32)],
        compiler_params=pltpu.CompilerParams(
            dimension_semantics=("arbitrary", "arbitrary", "arbitrary"),
            vmem_limit_bytes=_vmem_limit([_nbytes((tk, tn), F32), _nbytes((tk, LANES), F32)],
                                         _nbytes((tk, tn), F32))),
        name="ada_gemv",
    )(c_col, ada_w, b3)


def _rope_kernel(pos_ref, inv_ref, cmask_ref, sgn_ref, cos_ref, sin_ref):
    ang = pos_ref[...].astype(F32) * inv_ref[...]
    cos_ref[...] = jnp.cos(ang) * cmask_ref[...]
    sin_ref[...] = jnp.sin(ang) * sgn_ref[...]


def _rope_call(positions):
    S = positions.shape[-1]
    half = ROPE_DIM // 2
    inv_freq = 1.0 / (ROPE_THETA ** (jnp.arange(0, ROPE_DIM, 2, dtype=F32) / ROPE_DIM))
    zeros = jnp.zeros((LANES - ROPE_DIM,), F32)
    inv = jnp.concatenate([inv_freq, inv_freq, zeros]).reshape(1, LANES)
    cmask = jnp.concatenate([jnp.ones((ROPE_DIM,), F32), zeros]).reshape(1, LANES)
    sgn = jnp.concatenate([-jnp.ones((half,), F32), jnp.ones((half,), F32), zeros]).reshape(1, LANES)
    tm = _tile(S, 1024)
    row = pl.BlockSpec((1, LANES), lambda i: (0, 0))
    out = pl.BlockSpec((tm, LANES), lambda i: (i, 0))
    return pl.pallas_call(
        _rope_kernel,
        grid=(S // tm,),
        in_specs=[pl.BlockSpec((tm, 1), lambda i: (i, 0)), row, row, row],
        out_specs=[out, out],
        out_shape=[jax.ShapeDtypeStruct((S, LANES), F32)] * 2,
        compiler_params=pltpu.CompilerParams(dimension_semantics=("arbitrary",)),
        name="rope_tables",
    )(positions.reshape(S, 1), inv, cmask, sgn)


def _rope_block(blk, cos_t, sin_t):
    half = ROPE_DIM // 2
    lane = lax.broadcasted_iota(jnp.int32, blk.shape, 1)
    swapped = jnp.where(lane < half, pltpu.roll(blk, LANES - half, 1), pltpu.roll(blk, half, 1))
    return blk * cos_t + swapped * sin_t


def _prenorm_kernel(x_ref, g_ref, sc_ref, sh_ref, o_ref):
    y = _rms(x_ref[...], g_ref[...])
    o_ref[...] = (y * (1.0 + sc_ref[...]) + sh_ref[...]).astype(o_ref.dtype)


def _prenorm_call(x, g, sc, sh):
    S, D = x.shape
    tm = _tile(S, 256)
    vec = pl.BlockSpec((1, D), lambda i: (0, 0))
    row = pl.BlockSpec((tm, D), lambda i: (i, 0))
    return pl.pallas_call(
        _prenorm_kernel,
        grid=(S // tm,),
        in_specs=[row, vec, vec, vec],
        out_specs=row,
        out_shape=jax.ShapeDtypeStruct((S, D), BF16),
        compiler_params=pltpu.CompilerParams(
            dimension_semantics=("arbitrary",),
            vmem_limit_bytes=_vmem_limit([_nbytes((tm, D), F32), _nbytes((tm, D), BF16)],
                                         3 * _nbytes((tm, D), F32))),
        name="prenorm",
    )(x, g.reshape(1, D), sc.reshape(1, D), sh.reshape(1, D))


def _resid_kernel(*refs, modulated):
    x_ref, y_ref, gate_ref, w_ref = refs[:4]
    n_out = len(modulated)
    n_par = sum(3 if m else 1 for m in modulated)
    par = refs[4:4 + n_par]
    xo_ref = refs[4 + n_par]
    outs = refs[5 + n_par:5 + n_par + n_out]
    xn = x_ref[...] + gate_ref[...] * _rms(y_ref[...], w_ref[...])
    xo_ref[...] = xn
    if n_out:
        inv = lax.rsqrt(jnp.mean(xn * xn, axis=-1, keepdims=True) + NORM_EPS)
        xh = xn * inv
        p = 0
        for m, o_ref in zip(modulated, outs):
            h = xh * par[p][...]
            if m:
                h = h * (1.0 + par[p + 1][...]) + par[p + 2][...]
            p += 3 if m else 1
            o_ref[...] = h.astype(o_ref.dtype)


def _resid_call(x, y, gate, w, norms):
    S, D = x.shape
    tm = _tile(S, 256)
    vec = pl.BlockSpec((1, D), lambda i: (0, 0))
    row = pl.BlockSpec((tm, D), lambda i: (i, 0))
    modulated = tuple(len(n) == 3 for n in norms)
    params = [p.reshape(1, D) for n in norms for p in n]
    n_out = len(norms)
    return pl.pallas_call(
        functools.partial(_resid_kernel, modulated=modulated),
        grid=(S // tm,),
        in_specs=[row, row, vec, vec] + [vec] * len(params),
        out_specs=[row] * (1 + n_out),
        out_shape=[jax.ShapeDtypeStruct((S, D), F32)] + [jax.ShapeDtypeStruct((S, D), BF16)] * n_out,
        compiler_params=pltpu.CompilerParams(
            dimension_semantics=("arbitrary",),
            vmem_limit_bytes=_vmem_limit([_nbytes((tm, D), F32)] * 3 + [_nbytes((tm, D), BF16)] * n_out,
                                         4 * _nbytes((tm, D), F32))),
        name="resid_norm",
    )(x, y, gate.reshape(1, D), w.reshape(1, D), *params)


def _mm_kernel(x_ref, w_ref, *rest, nk, n_extra, n_out, epilogue):
    extras = rest[:n_extra]
    outs = rest[n_extra:n_extra + n_out]
    prod = jnp.dot(x_ref[...], w_ref[...], preferred_element_type=F32)
    if nk == 1:
        epilogue(prod, extras, outs)
        return
    acc_ref = rest[n_extra + n_out]
    k = pl.program_id(2)

    @pl.when(k == 0)
    def _():
        acc_ref[...] = prod

    @pl.when(k > 0)
    def _():
        acc_ref[...] += prod

    @pl.when(k == nk - 1)
    def _():
        epilogue(acc_ref[...], extras, outs)


def _mm_call(x, w, epilogue, outs, *, extras=(), tm=1024, tn=512, tk=4096, name="matmul"):
    M, K = x.shape
    N = w.shape[1]
    tm, tn, tk = _tile(M, tm), _tile(N, tn), _tile(K, tk)
    nk = K // tk
    in_specs = [pl.BlockSpec((tm, tk), lambda i, j, k: (i, k)),
                pl.BlockSpec((tk, tn), lambda i, j, k: (k, j))]
    blocks = [_nbytes((tm, tk), x.dtype), _nbytes((tk, tn), w.dtype)]
    arrays = []
    for arr, kind in extras:
        if kind == "col":
            in_specs.append(pl.BlockSpec((1, tn), lambda i, j, k: (0, j)))
            blocks.append(_nbytes((SUBLANES, tn), arr.dtype))
        elif kind == "row":
            in_specs.append(pl.BlockSpec((tm, arr.shape[1]), lambda i, j, k: (i, 0)))
            blocks.append(_nbytes((tm, arr.shape[1]), arr.dtype))
        else:
            in_specs.append(pl.BlockSpec(arr.shape, lambda i, j, k: (0,) * arr.ndim))
            blocks.append(_nbytes(arr.shape, arr.dtype))
        arrays.append(arr)
    out_specs, out_shape = [], []
    for n_tot, n_blk, dt in outs:
        n_blk = tn if n_blk is None else n_blk
        out_specs.append(pl.BlockSpec((tm, n_blk), lambda i, j, k: (i, j)))
        out_shape.append(jax.ShapeDtypeStruct((M, n_tot), dt))
        blocks.append(_nbytes((tm, n_blk), dt))
    scratch = [pltpu.VMEM((tm, tn), F32)] if nk > 1 else []
    return pl.pallas_call(
        functools.partial(_mm_kernel, nk=nk, n_extra=len(arrays), n_out=len(outs), epilogue=epilogue),
        grid=(M // tm, N // tn, nk),
        in_specs=in_specs,
        out_specs=out_specs,
        out_shape=out_shape,
        scratch_shapes=scratch,
        compiler_params=pltpu.CompilerParams(
            dimension_semantics=("arbitrary", "arbitrary", "arbitrary"),
            vmem_limit_bytes=_vmem_limit(blocks, 3 * _nbytes((tm, tn), F32))),
        name=name,
    )(x, w, *arrays)


def _ep_f32(acc, extras, outs):
    outs[0][...] = acc


def _ep_bf16(acc, extras, outs):
    outs[0][...] = acc.astype(BF16)


def _ep_silu(acc, extras, outs):
    outs[0][...] = _silu(acc).astype(BF16)


def _ep_forget(acc, extras, outs):
    lb = extras[0][...]
    f = lb + (1.0 - lb) * _sigmoid(acc)
    outs[0][...] = jnp.log(f)
    outs[1][...] = (1.0 - f).astype(BF16)


def _ep_rms(acc, extras, outs):
    outs[0][...] = _rms(acc, extras[0][...]).astype(BF16)


def _ep_q_rope(acc, extras, outs):
    cos_t, sin_t = extras[0][...], extras[1][...]
    for h in range(acc.shape[1] // Q_HEAD_PAD):
        lo = h * Q_HEAD_PAD
        outs[0][:, lo:lo + LANES] = (acc[:, lo:lo + LANES] * ATTN_SCALE).astype(BF16)
        pe = _rope_block(acc[:, lo + LANES:lo + Q_HEAD_PAD], cos_t, sin_t)
        outs[0][:, lo + LANES:lo + Q_HEAD_PAD] = (pe * ATTN_SCALE).astype(BF16)


def _ep_dkv(acc, extras, outs):
    g, cos_t, sin_t = extras[0][...], extras[1][...], extras[2][...]
    outs[0][...] = _rms(acc[:, :KV_LORA], g).astype(BF16)
    outs[1][...] = _rope_block(acc[:, KV_LORA:KV_LORA + LANES], cos_t, sin_t).astype(BF16)


def _ffn_up_kernel(x_ref, wg_ref, wu_ref, o_ref):
    x = x_ref[...]
    g = jnp.dot(x, wg_ref[...], preferred_element_type=F32)
    u = jnp.dot(x, wu_ref[...], preferred_element_type=F32)
    o_ref[...] = (_silu(g) * u).astype(o_ref.dtype)


def _ffn_up_call(h, wg, wu, *, tm=1024, tn=512):
    M, K = h.shape
    N = wg.shape[1]
    tm, tn = _tile(M, tm), _tile(N, tn)
    wspec = pl.BlockSpec((K, tn), lambda i, j: (0, j))
    return pl.pallas_call(
        _ffn_up_kernel,
        grid=(M // tm, N // tn),
        in_specs=[pl.BlockSpec((tm, K), lambda i, j: (i, 0)), wspec, wspec],
        out_specs=pl.BlockSpec((tm, tn), lambda i, j: (i, j)),
        out_shape=jax.ShapeDtypeStruct((M, N), BF16),
        compiler_params=pltpu.CompilerParams(
            dimension_semantics=("arbitrary", "arbitrary"),
            vmem_limit_bytes=_vmem_limit([_nbytes((tm, K), BF16), 2 * _nbytes((K, tn), BF16),
                                          _nbytes((tm, tn), BF16)], 4 * _nbytes((tm, tn), F32))),
        name="ffn_up",
    )(h, wg, wu)


def _boundary_cumsum(b, m):
    T, W = b.shape
    if 2 * m >= SUBLANES:
        nb = T // (2 * m)
        b3 = b.reshape(nb, 2 * m, W)
        return jnp.broadcast_to(b3[:, m - 1:m, :], (nb, 2 * m, W)).reshape(T, W)
    b3 = b.reshape(T // SUBLANES, SUBLANES, W)
    sub = lax.broadcasted_iota(jnp.int32, b3.shape, 1)
    r = jnp.broadcast_to(b3[:, m - 1:m, :], b3.shape)
    for blk in range(1, SUBLANES // (2 * m)):
        row = blk * 2 * m + m - 1
        r = jnp.where(sub >= blk * 2 * m, jnp.broadcast_to(b3[:, row:row + 1, :], b3.shape), r)
    return r.reshape(T, W)


def _hgrn_kernel(q_ref, k_ref, v_ref, lf_ref, gs_ref, ng_ref, tril_ref, rowsel_ref, pmask_ref,
                 o_ref, st_ref, *, n_chunks):
    T = HG_CHUNK

    @pl.when(pl.program_id(1) == 0)
    def _():
        st_ref[...] = jnp.zeros_like(st_ref)

    nt = (((1,), (1,)), ((), ()))
    tn = (((0,), (0,)), ((), ()))

    def chunk(c, carry):
        rows = pl.ds(pl.multiple_of(c * T, T), T)
        lf = lf_ref[rows, :]
        hi = lf.astype(BF16)
        r1 = lf - hi.astype(F32)
        mid = r1.astype(BF16)
        lo = (r1 - mid.astype(F32)).astype(BF16)
        parts = jnp.dot(tril_ref[...], jnp.concatenate([hi, mid, lo], axis=1),
                        preferred_element_type=F32)
        b = parts[:, :LANES] + parts[:, LANES:2 * LANES] + parts[:, 2 * LANES:]
        qb, kb, vb = q_ref[rows, :], k_ref[rows, :], v_ref[rows, :]
        q, k = qb.astype(F32), kb.astype(F32)
        st = st_ref[...]
        b_last = b[T - 1:T, :]
        o = lax.dot_general((q * jnp.exp(b)).astype(BF16), st.astype(BF16), nt,
                            preferred_element_type=F32)
        a = jnp.where(pmask_ref[HG_LEVELS] > 0.5,
                      lax.dot_general(qb, kb, nt, preferred_element_type=F32), 0.0)
        for lvl in range(HG_LEVELS):
            r = _boundary_cumsum(b, 1 << lvl)
            second = rowsel_ref[lvl] > 0.5
            arg = jnp.where(second, b - r, r - b)
            x = (jnp.where(second, q, k) * jnp.exp(arg)).astype(BF16)
            p = lax.dot_general(x, x, nt, preferred_element_type=F32)
            a = jnp.where(pmask_ref[lvl] > 0.5, p, a)
        o = o + jnp.dot(a.astype(BF16), vb, preferred_element_type=F32)
        kt = (k * jnp.exp(b_last - b)).astype(BF16)
        st_ref[...] = st * jnp.exp(b_last) + lax.dot_general(vb, kt, tn, preferred_element_type=F32)
        y = _rms(o, ng_ref[...]) * gs_ref[rows, :].astype(F32)
        o_ref[rows, :] = y.astype(o_ref.dtype)
        return carry

    lax.fori_loop(0, n_chunks, chunk, 0)


def _hgrn_masks():
    T = HG_CHUNK
    t = jnp.arange(T)
    tril = (t[:, None] >= t[None, :]).astype(BF16)
    rowsel, pmask = [], []
    for lvl in range(HG_LEVELS):
        m = 1 << lvl
        second = (t // m) % 2 == 1
        rowsel.append(jnp.broadcast_to(second[:, None], (T, HG_HEAD_DIM)))
        same = (t[:, None] // (2 * m)) == (t[None, :] // (2 * m))
        pmask.append(same & second[:, None] & (~second)[None, :])
    pmask.append(t[:, None] == t[None, :])
    return tril, jnp.stack(rowsel).astype(F32), jnp.stack(pmask).astype(F32)


def _hgrn_call(q, k, v, lf, gs, norm_g, *, t_blk=1024):
    S, D = q.shape
    H = D // HG_HEAD_DIM
    t_blk = _tile(S, t_blk)
    assert t_blk % HG_CHUNK == 0
    tril, rowsel, pmask = _hgrn_masks()
    seq = pl.BlockSpec((t_blk, HG_HEAD_DIM), lambda h, s: (s, h))

    def const(a):
        return pl.BlockSpec(a.shape, lambda h, s: (0,) * a.ndim)

    ng = norm_g.reshape(1, HG_HEAD_DIM)
    return pl.pallas_call(
        functools.partial(_hgrn_kernel, n_chunks=t_blk // HG_CHUNK),
        grid=(H, S // t_blk),
        in_specs=[seq, seq, seq, seq, seq, const(ng), const(tril), const(rowsel), const(pmask)],
        out_specs=seq,
        out_shape=jax.ShapeDtypeStruct((S, D), BF16),
        scratch_shapes=[pltpu.VMEM((HG_HEAD_DIM, HG_HEAD_DIM), F32)],
        compiler_params=pltpu.CompilerParams(dimension_semantics=("arbitrary", "arbitrary")),
        name="hgrn2_recurrence",
    )(q, k, v, lf, gs, ng, tril, rowsel, pmask)


def _attn_kernel(q_ref, kn_ref, v_ref, kpe_ref, o_ref, m_ref, l_ref, acc_ref, *, tq):
    qi = pl.program_id(1)
    q = q_ref[...]
    nt = (((1,), (1,)), ((), ()))
    m_ref[...] = jnp.full_like(m_ref, -jnp.inf)
    l_ref[...] = jnp.zeros_like(l_ref)
    acc_ref[...] = jnp.zeros_like(acc_ref)

    def step(j, masked):
        rows = pl.ds(pl.multiple_of(j * tq, tq), tq)
        kc = jnp.concatenate([kn_ref[rows, :], kpe_ref[rows, :]], axis=1)
        s = lax.dot_general(q, kc, nt, preferred_element_type=F32)
        if masked:
            r = lax.broadcasted_iota(jnp.int32, s.shape, 0)
            c = lax.broadcasted_iota(jnp.int32, s.shape, 1)
            s = jnp.where(r >= c, s, -jnp.inf)
        m_old = m_ref[...]
        m_new = jnp.maximum(m_old, s.max(axis=-1, keepdims=True))
        alpha = jnp.exp(m_old - m_new)
        p = jnp.exp(s - m_new)
        l_ref[...] = alpha * l_ref[...] + p.sum(axis=-1, keepdims=True)
        acc_ref[...] = alpha * acc_ref[...] + jnp.dot(p.astype(BF16), v_ref[rows, :],
                                                      preferred_element_type=F32)
        m_ref[...] = m_new

    def body(j, carry):
        step(j, False)
        return carry

    lax.fori_loop(0, qi, body, 0)
    step(qi, True)
    o_ref[...] = (acc_ref[...] / l_ref[...]).astype(o_ref.dtype)


def _attn_call(q, kv, kpe, *, tq=512):
    S = q.shape[0]
    H = q.shape[1] // Q_HEAD_PAD
    tq = _tile(S, tq)
    return pl.pallas_call(
        functools.partial(_attn_kernel, tq=tq),
        grid=(H, S // tq),
        in_specs=[
            pl.BlockSpec((tq, Q_HEAD_PAD), lambda h, i: (i, h)),
            pl.BlockSpec((S, NOPE_DIM), lambda h, i: (0, 2 * h)),
            pl.BlockSpec((S, V_HEAD_DIM), lambda h, i: (0, 2 * h + 1)),
            pl.BlockSpec((S, LANES), lambda h, i: (0, 0)),
        ],
        out_specs=pl.BlockSpec((tq, V_HEAD_DIM), lambda h, i: (i, h)),
        out_shape=jax.ShapeDtypeStruct((S, H * V_HEAD_DIM), BF16),
        scratch_shapes=[pltpu.VMEM((tq, 1), F32), pltpu.VMEM((tq, 1), F32),
                        pltpu.VMEM((tq, V_HEAD_DIM), F32)],
        compiler_params=pltpu.CompilerParams(
            dimension_semantics=("arbitrary", "arbitrary"),
            vmem_limit_bytes=_vmem_limit([3 * _nbytes((S, LANES), BF16), _nbytes((tq, Q_HEAD_PAD), BF16)],
                                         6 * _nbytes((tq, tq), F32))),
        name="mla_attention",
    )(q, kv, kv, kpe)


def _pad_cols(w, n):
    return jnp.pad(w, ((0, 0), (0, n - w.shape[1])))


def _swiglu(h, w_gate, w_up, w_down):
    F = w_gate.shape[1]
    Fp = -(-F // 1024) * 1024 if F > 1024 else F
    wg = _pad_cols(w_gate, Fp).astype(BF16)
    wu = _pad_cols(w_up, Fp).astype(BF16)
    wd = jnp.pad(w_down, ((0, Fp - F), (0, 0))).astype(BF16)
    a = _ffn_up_call(h, wg, wu)
    (y,) = _mm_call(a, wd, _ep_f32, [(wd.shape[1], None, F32)],
                    tk=Fp // 4 if Fp % (4 * 256) == 0 else Fp, name="ffn_down")
    return y


def kernel(x, c, positions, ada_w, ada_b, norm_g, ffn_w_gate, ffn_w_up, ffn_w_down, hg_w_in, hg_lb_logits, hg_norm_g, hg_w_out, mla_w_dq, mla_q_norm_g, mla_w_uq, mla_w_o, kv_norm_in_g, kv_w_dkv, kv_norm_g, kv_w_ukv):
    B, S, D = x.shape
    assert B == 1
    depth = ada_w.shape[0]
    n_a = hg_w_in.shape[0]
    mla_heads = kv_w_ukv.shape[1] // (NOPE_DIM + V_HEAD_DIM)
    fdim = hg_lb_logits.shape[1]

    xs = x.reshape(S, D)
    mod = _ada_call(c, ada_w, ada_b).reshape(depth, 6, D)
    cos_t, sin_t = _rope_call(positions)
    lower_bounds = jnp.cumsum(jax.nn.softmax(hg_lb_logits.astype(F32), axis=0), axis=0)

    h = _prenorm_call(xs, norm_g[0, 0], mod[0, 1], mod[0, 0])
    kv = kpe = None
    for l in range(depth):
        sh_m, sc_m, g_m, sh_f, sc_f, g_f = (mod[l, i] for i in range(6))
        if l < n_a:
            w_in = hg_w_in[l]
            lb = lower_bounds[l].reshape(1, fdim)
            (q,) = _mm_call(h, w_in[:, :fdim].astype(BF16), _ep_silu, [(fdim, None, BF16)], name="hg_q")
            lf, k = _mm_call(h, w_in[:, fdim:2 * fdim].astype(BF16), _ep_forget,
                             [(fdim, None, F32), (fdim, None, BF16)], extras=[(lb, "col")], name="hg_f")
            (v,) = _mm_call(h, w_in[:, 2 * fdim:2 * fdim + D].astype(BF16), _ep_bf16,
                            [(D, None, BF16)], name="hg_i")
            (gs,) = _mm_call(h, w_in[:, 2 * fdim + D:].astype(BF16), _ep_silu, [(D, None, BF16)], name="hg_g")
            o = _hgrn_call(q, k, v, lf, gs, hg_norm_g[l])
            (y,) = _mm_call(o, hg_w_out[l].astype(BF16), _ep_f32, [(D, None, F32)], name="hg_out")
        else:
            j = l - n_a
            q_lora = mla_w_dq.shape[2]
            (qd,) = _mm_call(h, mla_w_dq[j].astype(BF16), _ep_rms, [(q_lora, q_lora, BF16)],
                             extras=[(mla_q_norm_g[j].reshape(1, q_lora), "col")], tn=q_lora, name="mla_dq")
            w_uq = mla_w_uq[j].reshape(q_lora, mla_heads, QK_HEAD_DIM)
            w_uq = jnp.pad(w_uq, ((0, 0), (0, 0), (0, Q_HEAD_PAD - QK_HEAD_DIM)))
            w_uq = w_uq.reshape(q_lora, mla_heads * Q_HEAD_PAD).astype(BF16)
            (q,) = _mm_call(qd, w_uq, _ep_q_rope, [(w_uq.shape[1], None, BF16)],
                            extras=[(cos_t, "row"), (sin_t, "row")], name="mla_uq")
            o = _attn_call(q, kv, kpe)
            (y,) = _mm_call(o, mla_w_o[j].astype(BF16), _ep_f32, [(D, None, F32)], name="mla_o")
        xs, h = _resid_call(xs, y, g_m, norm_g[l, 1], [(norm_g[l, 2], sc_f, sh_f)])
        y = _swiglu(h, ffn_w_gate[l], ffn_w_up[l], ffn_w_down[l])
        norms = []
        if l + 1 < depth:
            norms.append((norm_g[l + 1, 0], mod[l + 1, 1], mod[l + 1, 0]))
        if l == n_a - 1:
            norms.append((kv_norm_in_g,))
        res = _resid_call(xs, y, g_f, norm_g[l, 3], norms)
        xs = res[0]
        if l + 1 < depth:
            h = res[1]
        if l == n_a - 1:
            src = res[-1]
            n_dkv = KV_LORA + LANES
            ckvn, kpe = _mm_call(src, _pad_cols(kv_w_dkv, n_dkv).astype(BF16), _ep_dkv,
                                 [(KV_LORA, KV_LORA, BF16), (LANES, LANES, BF16)],
                                 extras=[(kv_norm_g.reshape(1, KV_LORA), "full"), (cos_t, "row"), (sin_t, "row")],
                                 tn=n_dkv, name="mla_dkv")
            (kv,) = _mm_call(ckvn, kv_w_ukv.astype(BF16), _ep_bf16, [(kv_w_ukv.shape[1], None, BF16)],
                             name="mla_ukv")
    return xs.reshape(B, S, D)
```

```python
import functools
import math

import jax
import jax.numpy as jnp
from jax import lax
from jax.experimental import pallas as pl
from jax.experimental.pallas import tpu as pltpu

F32 = jnp.float32
BF16 = jnp.bfloat16

HG_HEAD_DIM = 128
NOPE_DIM = 128
ROPE_DIM = 64
V_HEAD_DIM = 128
QK_HEAD_DIM = NOPE_DIM + ROPE_DIM
KV_LORA = 512
ROPE_THETA = 10000.0
NORM_EPS = 1e-6
ATTN_SCALE = 1.0 / math.sqrt(QK_HEAD_DIM)
Q_SCALE_LOG2 = ATTN_SCALE * math.log2(math.e)

LANES = 128
SUBLANES = 8
VMEM_BYTES_V7X = 64 * 1024 * 1024
VMEM_LIMIT_CAP = VMEM_BYTES_V7X - 8 * 1024 * 1024
VMEM_LIMIT_FLOOR = 32 * 1024 * 1024

Q_HEAD_PAD = 2 * LANES
HG_CHUNK = 128
HG_LEVELS = 7


def _vmem_limit(block_bytes, temp_bytes=0):
    est = 2 * sum(block_bytes) + temp_bytes + (4 << 20)
    return int(min(max(est, VMEM_LIMIT_FLOOR), VMEM_LIMIT_CAP))


def _nbytes(shape, dtype):
    return math.prod(shape) * jnp.dtype(dtype).itemsize


def _tile(dim, pref):
    if dim <= pref:
        return dim
    t = (pref // LANES) * LANES
    while t >= LANES:
        if dim % t == 0:
            return t
        t -= LANES
    return dim


def _sigmoid(x):
    return 1.0 / (1.0 + jnp.exp(-x))


def _silu(x):
    return x * _sigmoid(x)


def _rms(x, g):
    ms = jnp.mean(x * x, axis=-1, keepdims=True)
    return x * lax.rsqrt(ms + NORM_EPS) * g


def _ada_kernel(c_ref, w_ref, b_ref, o_ref, acc_ref, *, nk):
    k = pl.program_id(2)
    tk, tn = w_ref.shape[1], w_ref.shape[2]
    p = w_ref[0] * _silu(c_ref[...])
    part = p.reshape(tk // SUBLANES, SUBLANES, tn).sum(axis=0)

    @pl.when(k == 0)
    def _():
        acc_ref[...] = part

    @pl.when(k > 0)
    def _():
        acc_ref[...] += part

    @pl.when(k == nk - 1)
    def _():
        o_ref[0] = acc_ref[...].sum(axis=0, keepdims=True) + b_ref[0]


def _ada_call(c, ada_w, ada_b):
    L, D, N = ada_w.shape
    tk, tn = _tile(D, 1024), _tile(N, 2048)
    nk = D // tk
    c_col = c.reshape(D, 1)
    b3 = ada_b.reshape(L, 1, N)
    return pl.pallas_call(
        functools.partial(_ada_kernel, nk=nk),
        grid=(L, N // tn, nk),
        in_specs=[
            pl.BlockSpec((tk, 1), lambda l, j, k: (k, 0)),
            pl.BlockSpec((1, tk, tn), lambda l, j, k: (l, k, j)),
            pl.BlockSpec((1, 1, tn), lambda l, j, k: (l, 0, j)),
        ],
        out_specs=pl.BlockSpec((1, 1, tn), lambda l, j, k: (l, 0, j)),
        out_shape=jax.ShapeDtypeStruct((L, 1, N), F32),
        scratch_shapes=[pltpu.VMEM((SUBLANES, tn), F32)],
        compiler_params=pltpu.CompilerParams(
            dimension_semantics=("arbitrary", "arbitrary", "arbitrary"),
            vmem_limit_bytes=_vmem_limit([_nbytes((tk, tn), F32), _nbytes((tk, LANES), F32)],
                                         _nbytes((tk, tn), F32))),
        name="ada_gemv",
    )(c_col, ada_w, b3)


def _rope_kernel(pos_ref, inv_ref, cmask_ref, sgn_ref, cos_ref, sin_ref):
    ang = pos_ref[...].astype(F32) * inv_ref[...]
    cos_ref[...] = jnp.cos(ang) * cmask_ref[...]
    sin_ref[...] = jnp.sin(ang) * sgn_ref[...]


def _rope_call(positions):
    S = positions.shape[-1]
    half = ROPE_DIM // 2
    inv_freq = 1.0 / (ROPE_THETA ** (jnp.arange(0, ROPE_DIM, 2, dtype=F32) / ROPE_DIM))
    zeros = jnp.zeros((LANES - ROPE_DIM,), F32)
    inv = jnp.concatenate([inv_freq, inv_freq, zeros]).reshape(1, LANES)
    cmask = jnp.concatenate([jnp.ones((ROPE_DIM,), F32), zeros]).reshape(1, LANES)
    sgn = jnp.concatenate([-jnp.ones((half,), F32), jnp.ones((half,), F32), zeros]).reshape(1, LANES)
    tm = _tile(S, 1024)
    row = pl.BlockSpec((1, LANES), lambda i: (0, 0))
    out = pl.BlockSpec((tm, LANES), lambda i: (i, 0))
    return pl.pallas_call(
        _rope_kernel,
        grid=(S // tm,),
        in_specs=[pl.BlockSpec((tm, 1), lambda i: (i, 0)), row, row, row],
        out_specs=[out, out],
        out_shape=[jax.ShapeDtypeStruct((S, LANES), F32)] * 2,
        compiler_params=pltpu.CompilerParams(dimension_semantics=("arbitrary",)),
        name="rope_tables",
    )(positions.reshape(S, 1), inv, cmask, sgn)


def _rope_block(blk, cos_t, sin_t):
    half = ROPE_DIM // 2
    lane = lax.broadcasted_iota(jnp.int32, blk.shape, 1)
    swapped = jnp.where(lane < half, pltpu.roll(blk, LANES - half, 1), pltpu.roll(blk, half, 1))
    return blk * cos_t + swapped * sin_t


def _prenorm_kernel(x_ref, g_ref, sc_ref, sh_ref, o_ref):
    y = _rms(x_ref[...], g_ref[...])
    o_ref[...] = (y * (1.0 + sc_ref[...]) + sh_ref[...]).astype(o_ref.dtype)


def _prenorm_call(x, g, sc, sh):
    S, D = x.shape
    tm = _tile(S, 256)
    vec = pl.BlockSpec((1, D), lambda i: (0, 0))
    row = pl.BlockSpec((tm, D), lambda i: (i, 0))
    return pl.pallas_call(
        _prenorm_kernel,
        grid=(S // tm,),
        in_specs=[row, vec, vec, vec],
        out_specs=row,
        out_shape=jax.ShapeDtypeStruct((S, D), BF16),
        compiler_params=pltpu.CompilerParams(
            dimension_semantics=("arbitrary",),
            vmem_limit_bytes=_vmem_limit([_nbytes((tm, D), F32), _nbytes((tm, D), BF16)],
                                         3 * _nbytes((tm, D), F32))),
        name="prenorm",
    )(x, g.reshape(1, D), sc.reshape(1, D), sh.reshape(1, D))


def _resid_kernel(*refs, modulated):
    x_ref, y_ref, gate_ref, w_ref = refs[:4]
    n_out = len(modulated)
    n_par = sum(3 if m else 1 for m in modulated)
    par = refs[4:4 + n_par]
    xo_ref = refs[4 + n_par]
    outs = refs[5 + n_par:5 + n_par + n_out]
    xn = x_ref[...] + gate_ref[...] * _rms(y_ref[...], w_ref[...])
    xo_ref[...] = xn
    if n_out:
        inv = lax.rsqrt(jnp.mean(xn * xn, axis=-1, keepdims=True) + NORM_EPS)
        xh = xn * inv
        p = 0
        for m, o_ref in zip(modulated, outs):
            h = xh * par[p][...]
            if m:
                h = h * (1.0 + par[p + 1][...]) + par[p + 2][...]
            p += 3 if m else 1
            o_ref[...] = h.astype(o_ref.dtype)


def _resid_call(x, y, gate, w, norms):
    S, D = x.shape
    tm = _tile(S, 256)
    vec = pl.BlockSpec((1, D), lambda i: (0, 0))
    row = pl.BlockSpec((tm, D), lambda i: (i, 0))
    modulated = tuple(len(n) == 3 for n in norms)
    params = [p.reshape(1, D) for n in norms for p in n]
    n_out = len(norms)
    return pl.pallas_call(
        functools.partial(_resid_kernel, modulated=modulated),
        grid=(S // tm,),
        in_specs=[row, row, vec, vec] + [vec] * len(params),
        out_specs=[row] * (1 + n_out),
        out_shape=[jax.ShapeDtypeStruct((S, D), F32)] + [jax.ShapeDtypeStruct((S, D), BF16)] * n_out,
        compiler_params=pltpu.CompilerParams(
            dimension_semantics=("arbitrary",),
            vmem_limit_bytes=_vmem_limit([_nbytes((tm, D), F32)] * 3 + [_nbytes((tm, D), BF16)] * n_out,
                                         4 * _nbytes((tm, D), F32))),
        name="resid_norm",
    )(x, y, gate.reshape(1, D), w.reshape(1, D), *params)


def _mm_kernel(x_ref, w_ref, *rest, nk, n_extra, n_out, epilogue):
    extras = rest[:n_extra]
    outs = rest[n_extra:n_extra + n_out]
    prod = jnp.dot(x_ref[...], w_ref[...], preferred_element_type=F32)
    if nk == 1:
        epilogue(prod, extras, outs)
        return
    acc_ref = rest[n_extra + n_out]
    k = pl.program_id(2)

    @pl.when(k == 0)
    def _():
        acc_ref[...] = prod

    @pl.when(k > 0)
    def _():
        acc_ref[...] += prod

    @pl.when(k == nk - 1)
    def _():
        epilogue(acc_ref[...], extras, outs)


def _mm_call(x, w, epilogue, outs, *, extras=(), tm=1024, tn=512, tk=4096, name="matmul"):
    M, K = x.shape
    N = w.shape[1]
    tm, tn, tk = _tile(M, tm), _tile(N, tn), _tile(K, tk)
    nk = K // tk
    in_specs = [pl.BlockSpec((tm, tk), lambda i, j, k: (i, k)),
                pl.BlockSpec((tk, tn), lambda i, j, k: (k, j))]
    blocks = [_nbytes((tm, tk), x.dtype), _nbytes((tk, tn), w.dtype)]
    arrays = []
    for arr, kind in extras:
        if kind == "col":
            in_specs.append(pl.BlockSpec((1, tn), lambda i, j, k: (0, j)))
            blocks.append(_nbytes((SUBLANES, tn), arr.dtype))
        elif kind == "row":
            in_specs.append(pl.BlockSpec((tm, arr.shape[1]), lambda i, j, k: (i, 0)))
            blocks.append(_nbytes((tm, arr.shape[1]), arr.dtype))
        else:
            in_specs.append(pl.BlockSpec(arr.shape, lambda i, j, k: (0,) * arr.ndim))
            blocks.append(_nbytes(arr.shape, arr.dtype))
        arrays.append(arr)
    out_specs, out_shape = [], []
    for n_tot, n_blk, dt in outs:
        n_blk = tn if n_blk is None else n_blk
        out_specs.append(pl.BlockSpec((tm, n_blk), lambda i, j, k: (i, j)))
        out_shape.append(jax.ShapeDtypeStruct((M, n_tot), dt))
        blocks.append(_nbytes((tm, n_blk), dt))
    scratch = [pltpu.VMEM((tm, tn), F32)] if nk > 1 else []
    return pl.pallas_call(
        functools.partial(_mm_kernel, nk=nk, n_extra=len(arrays), n_out=len(outs), epilogue=epilogue),
        grid=(M // tm, N // tn, nk),
        in_specs=in_specs,
        out_specs=out_specs,
        out_shape=out_shape,
        scratch_shapes=scratch,
        compiler_params=pltpu.CompilerParams(
            dimension_semantics=("arbitrary", "arbitrary", "arbitrary"),
            vmem_limit_bytes=_vmem_limit(blocks, 2 * _nbytes((tm, tn), F32))),
        name=name,
    )(x, w, *arrays)


def _ep_f32(acc, extras, outs):
    outs[0][...] = acc


def _ep_bf16(acc, extras, outs):
    outs[0][...] = acc.astype(BF16)


def _ep_silu(acc, extras, outs):
    outs[0][...] = _silu(acc).astype(BF16)


def _ep_forget(acc, extras, outs):
    lb = extras[0][...]
    f = lb + (1.0 - lb) * _sigmoid(acc)
    outs[0][...] = jnp.log2(f)
    outs[1][...] = (1.0 - f).astype(BF16)


def _ep_rms(acc, extras, outs):
    outs[0][...] = _rms(acc, extras[0][...]).astype(BF16)


def _ep_q_rope(acc, extras, outs):
    cos_t, sin_t = extras[0][...], extras[1][...]
    for h in range(acc.shape[1] // Q_HEAD_PAD):
        lo = h * Q_HEAD_PAD
        outs[0][:, lo:lo + LANES] = (acc[:, lo:lo + LANES] * Q_SCALE_LOG2).astype(BF16)
        pe = _rope_block(acc[:, lo + LANES:lo + Q_HEAD_PAD], cos_t, sin_t)
        outs[0][:, lo + LANES:lo + Q_HEAD_PAD] = (pe * Q_SCALE_LOG2).astype(BF16)


def _ep_dkv(acc, extras, outs):
    g, cos_t, sin_t = extras[0][...], extras[1][...], extras[2][...]
    outs[0][...] = _rms(acc[:, :KV_LORA], g).astype(BF16)
    outs[1][...] = _rope_block(acc[:, KV_LORA:KV_LORA + LANES], cos_t, sin_t).astype(BF16)


def _ffn_up_kernel(x_ref, wg_ref, wu_ref, o_ref):
    x = x_ref[...]
    g = jnp.dot(x, wg_ref[...], preferred_element_type=F32)
    u = jnp.dot(x, wu_ref[...], preferred_element_type=F32)
    o_ref[...] = (_silu(g) * u).astype(o_ref.dtype)


def _ffn_up_call(h, wg, wu, *, tm=1024, tn=512):
    M, K = h.shape
    N = wg.shape[1]
    tm, tn = _tile(M, tm), _tile(N, tn)
    wspec = pl.BlockSpec((K, tn), lambda i, j: (0, j))
    return pl.pallas_call(
        _ffn_up_kernel,
        grid=(M // tm, N // tn),
        in_specs=[pl.BlockSpec((tm, K), lambda i, j: (i, 0)), wspec, wspec],
        out_specs=pl.BlockSpec((tm, tn), lambda i, j: (i, j)),
        out_shape=jax.ShapeDtypeStruct((M, N), BF16),
        compiler_params=pltpu.CompilerParams(
            dimension_semantics=("arbitrary", "arbitrary"),
            vmem_limit_bytes=_vmem_limit([_nbytes((tm, K), BF16), 2 * _nbytes((K, tn), BF16),
                                          _nbytes((tm, tn), BF16)], 4 * _nbytes((tm, tn), F32))),
        name="ffn_up",
    )(h, wg, wu)


def _boundary_cumsum(b, m):
    T, W = b.shape
    if 2 * m >= SUBLANES:
        nb = T // (2 * m)
        b3 = b.reshape(nb, 2 * m, W)
        return jnp.broadcast_to(b3[:, m - 1:m, :], (nb, 2 * m, W)).reshape(T, W)
    b3 = b.reshape(T // SUBLANES, SUBLANES, W)
    sub = lax.broadcasted_iota(jnp.int32, b3.shape, 1)
    r = jnp.broadcast_to(b3[:, m - 1:m, :], b3.shape)
    for blk in range(1, SUBLANES // (2 * m)):
        row = blk * 2 * m + m - 1
        r = jnp.where(sub >= blk * 2 * m, jnp.broadcast_to(b3[:, row:row + 1, :], b3.shape), r)
    return r.reshape(T, W)


def _hgrn_kernel(q_ref, k_ref, v_ref, lf_ref, gs_ref, ng_ref, tril_ref, sgn_ref, rowsel_ref, pmask_ref,
                 o_ref, st_ref, *, n_chunks, nh):
    T = HG_CHUNK

    @pl.when(pl.program_id(1) == 0)
    def _():
        st_ref[...] = jnp.zeros_like(st_ref)

    nt = (((1,), (1,)), ((), ()))
    tn = (((0,), (0,)), ((), ()))
    heads = range(nh)
    cols = [slice(hh * HG_HEAD_DIM, (hh + 1) * HG_HEAD_DIM) for hh in heads]

    def chunk(c, carry):
        rows = pl.ds(pl.multiple_of(c * T, T), T)
        lf = [lf_ref[rows, cs] for cs in cols]
        b = []
        for hh in heads:
            hi = lf[hh].astype(BF16)
            r1 = lf[hh] - hi.astype(F32)
            mid = r1.astype(BF16)
            lo = (r1 - mid.astype(F32)).astype(BF16)
            parts = jnp.dot(tril_ref[...], jnp.concatenate([hi, mid, lo], axis=1),
                            preferred_element_type=F32)
            b.append(parts[:, :LANES] + parts[:, LANES:2 * LANES] + parts[:, 2 * LANES:])
        qb = [q_ref[rows, cs] for cs in cols]
        kb = [k_ref[rows, cs] for cs in cols]
        vb = [v_ref[rows, cs] for cs in cols]
        q = [t.astype(F32) for t in qb]
        k = [t.astype(F32) for t in kb]
        st = [st_ref[hh] for hh in heads]
        o = [lax.dot_general((q[hh] * jnp.exp2(b[hh])).astype(BF16), st[hh].astype(BF16), nt,
                             preferred_element_type=F32) for hh in heads]
        a = [jnp.where(pmask_ref[HG_LEVELS] > 0.5,
                       lax.dot_general(qb[hh], kb[hh], nt, preferred_element_type=F32), 0.0)
             for hh in heads]
        for lvl in range(HG_LEVELS):
            second = rowsel_ref[lvl] > 0.5
            for hh in heads:
                if lvl == 0:
                    arg = lf[hh] * rowsel_ref[0]
                else:
                    arg = (b[hh] - _boundary_cumsum(b[hh], 1 << lvl)) * sgn_ref[lvl]
                x = (jnp.where(second, q[hh], k[hh]) * jnp.exp2(arg)).astype(BF16)
                p = lax.dot_general(x, x, nt, preferred_element_type=F32)
                a[hh] = jnp.where(pmask_ref[lvl] > 0.5, p, a[hh])
        for hh in heads:
            o[hh] = o[hh] + jnp.dot(a[hh].astype(BF16), vb[hh], preferred_element_type=F32)
        for hh in heads:
            b_last = b[hh][T - 1:T, :]
            kt = (k[hh] * jnp.exp2(b_last - b[hh])).astype(BF16)
            st_ref[hh] = st[hh] * jnp.exp2(b_last) + lax.dot_general(vb[hh], kt, tn,
                                                                     preferred_element_type=F32)
        for hh in heads:
            y = _rms(o[hh], ng_ref[...]) * gs_ref[rows, cols[hh]].astype(F32)
            o_ref[rows, cols[hh]] = y.astype(o_ref.dtype)
        return carry

    lax.fori_loop(0, n_chunks, chunk, 0)


def _hgrn_masks():
    T = HG_CHUNK
    t = jnp.arange(T)
    tril = (t[:, None] >= t[None, :]).astype(BF16)
    rowsel, pmask = [], []
    for lvl in range(HG_LEVELS):
        m = 1 << lvl
        second = (t // m) % 2 == 1
        rowsel.append(jnp.broadcast_to(second[:, None], (T, HG_HEAD_DIM)))
        same = (t[:, None] // (2 * m)) == (t[None, :] // (2 * m))
        pmask.append(same & second[:, None] & (~second)[None, :])
    pmask.append(t[:, None] == t[None, :])
    rowsel = jnp.stack(rowsel).astype(F32)
    return tril, 2.0 * rowsel - 1.0, rowsel, jnp.stack(pmask).astype(F32)


def _hgrn_call(q, k, v, lf2, gs, norm_g, *, t_blk=1024, nh=8):
    S, D = q.shape
    H = D // HG_HEAD_DIM
    t_blk = _tile(S, t_blk)
    nh = math.gcd(nh, H)
    assert t_blk % HG_CHUNK == 0
    tril, sgn, rowsel, pmask = _hgrn_masks()
    w_blk = nh * HG_HEAD_DIM
    seq = pl.BlockSpec((t_blk, w_blk), lambda h, s: (s, h))

    def const(a):
        return pl.BlockSpec(a.shape, lambda h, s: (0,) * a.ndim)

    ng = norm_g.reshape(1, HG_HEAD_DIM)
    return pl.pallas_call(
        functools.partial(_hgrn_kernel, n_chunks=t_blk // HG_CHUNK, nh=nh),
        grid=(H // nh, S // t_blk),
        in_specs=[seq, seq, seq, seq, seq, const(ng), const(tril), const(sgn), const(rowsel), const(pmask)],
        out_specs=seq,
        out_shape=jax.ShapeDtypeStruct((S, D), BF16),
        scratch_shapes=[pltpu.VMEM((nh, HG_HEAD_DIM, HG_HEAD_DIM), F32)],
        compiler_params=pltpu.CompilerParams(
            dimension_semantics=("arbitrary", "arbitrary"),
            vmem_limit_bytes=_vmem_limit([5 * _nbytes((t_blk, w_blk), BF16), _nbytes((t_blk, w_blk), F32)],
                                         64 * nh * _nbytes((HG_CHUNK, HG_HEAD_DIM), F32))),
        name="hgrn2_recurrence",
    )(q, k, v, lf2, gs, ng, tril, sgn, rowsel, pmask)


def _attn_kernel(q_ref, kn_ref, v_ref, kpe_ref, o_ref, m_ref, acc_ref, *, tq, tk, nsub):
    qi = pl.program_id(1)
    tb = tq * nsub
    kv_per_block = tb // tk
    nt = (((1,), (1,)), ((), ()))
    m_ref[...] = jnp.full_like(m_ref, -jnp.inf)
    acc_ref[...] = jnp.zeros_like(acc_ref)
    ones = jnp.ones((tk, LANES), BF16)

    def tile_update(a, kc, ve, mask_off):
        q = q_ref[a * tq:(a + 1) * tq, :]
        s = lax.dot_general(q, kc, nt, preferred_element_type=F32)
        if mask_off is not None:
            r = lax.broadcasted_iota(jnp.int32, s.shape, 0)
            c = lax.broadcasted_iota(jnp.int32, s.shape, 1)
            s = jnp.where(r + mask_off >= c, s, -jnp.inf)
        m_old = m_ref[a]
        m_new = jnp.maximum(m_old, s.max(axis=-1, keepdims=True))
        alpha = jnp.exp2(m_old - m_new)
        p = jnp.concatenate([jnp.exp2(s[:, cb * LANES:(cb + 1) * LANES] - m_new)
                             for cb in range(tk // LANES)], axis=1).astype(BF16)
        pv = jnp.dot(p, ve, preferred_element_type=F32)
        acc_ref[a] = jnp.concatenate([alpha, alpha], axis=1) * acc_ref[a] + pv
        m_ref[a] = m_new

    def load_kv(j):
        rows = pl.ds(pl.multiple_of(j * tk, tk), tk)
        kc = jnp.concatenate([kn_ref[rows, :], kpe_ref[rows, :]], axis=1)
        ve = jnp.concatenate([v_ref[rows, :], ones], axis=1)
        return kc, ve

    def body(j, carry):
        kc, ve = load_kv(j)
        for a in range(nsub):
            tile_update(a, kc, ve, None)
        return carry

    lax.fori_loop(0, qi * kv_per_block, body, 0)
    for b in range(kv_per_block):
        kc, ve = load_kv(qi * kv_per_block + b)
        for a in range(nsub):
            if b * tk > (a + 1) * tq - 1:
                continue
            unmasked = (b + 1) * tk - 1 <= a * tq
            tile_update(a, kc, ve, None if unmasked else a * tq - b * tk)
    for a in range(nsub):
        acc = acc_ref[a]
        o_ref[a * tq:(a + 1) * tq, :] = (acc[:, :V_HEAD_DIM] / acc[:, V_HEAD_DIM:]).astype(o_ref.dtype)


def _attn_call(q, kv, kpe, *, tq=256, tk=1024, nsub=8):
    S = q.shape[0]
    H = q.shape[1] // Q_HEAD_PAD
    tk = _tile(S, tk)
    nsub = min(nsub, S // tq)
    tb = tq * nsub
    assert S % tb == 0 and tb % tk == 0
    return pl.pallas_call(
        functools.partial(_attn_kernel, tq=tq, tk=tk, nsub=nsub),
        grid=(H, S // tb),
        in_specs=[
            pl.BlockSpec((tb, Q_HEAD_PAD), lambda h, i: (i, h)),
            pl.BlockSpec((S, NOPE_DIM), lambda h, i: (0, 2 * h)),
            pl.BlockSpec((S, V_HEAD_DIM), lambda h, i: (0, 2 * h + 1)),
            pl.BlockSpec((S, LANES), lambda h, i: (0, 0)),
        ],
        out_specs=pl.BlockSpec((tb, V_HEAD_DIM), lambda h, i: (i, h)),
        out_shape=jax.ShapeDtypeStruct((S, H * V_HEAD_DIM), BF16),
        scratch_shapes=[pltpu.VMEM((nsub, tq, LANES), F32), pltpu.VMEM((nsub, tq, 2 * LANES), F32)],
        compiler_params=pltpu.CompilerParams(
            dimension_semantics=("arbitrary", "arbitrary"),
            vmem_limit_bytes=_vmem_limit([3 * _nbytes((S, LANES), BF16), _nbytes((tb, Q_HEAD_PAD), BF16),
                                          _nbytes((tb, V_HEAD_DIM), BF16)],
                                         3 * nsub * _nbytes((tq, tk), F32))),
        name="mla_attention",
    )(q, kv, kv, kpe)


def _pad_cols(w, n):
    return jnp.pad(w, ((0, 0), (0, n - w.shape[1])))


def _swiglu(h, w_gate, w_up, w_down):
    F = w_gate.shape[1]
    Fp = -(-F // 1024) * 1024 if F > 1024 else F
    wg = _pad_cols(w_gate, Fp).astype(BF16)
    wu = _pad_cols(w_up, Fp).astype(BF16)
    wd = jnp.pad(w_down, ((0, Fp - F), (0, 0))).astype(BF16)
    a = _ffn_up_call(h, wg, wu)
    (y,) = _mm_call(a, wd, _ep_f32, [(wd.shape[1], None, F32)], tn=1024,
                    tk=Fp // 4 if Fp % (4 * 256) == 0 else Fp, name="ffn_down")
    return y


def kernel(x, c, positions, ada_w, ada_b, norm_g, ffn_w_gate, ffn_w_up, ffn_w_down, hg_w_in, hg_lb_logits, hg_norm_g, hg_w_out, mla_w_dq, mla_q_norm_g, mla_w_uq, mla_w_o, kv_norm_in_g, kv_w_dkv, kv_norm_g, kv_w_ukv):
    B, S, D = x.shape
    assert B == 1
    depth = ada_w.shape[0]
    n_a = hg_w_in.shape[0]
    mla_heads = kv_w_ukv.shape[1] // (NOPE_DIM + V_HEAD_DIM)
    fdim = hg_lb_logits.shape[1]

    xs = x.reshape(S, D)
    mod = _ada_call(c, ada_w, ada_b).reshape(depth, 6, D)
    cos_t, sin_t = _rope_call(positions)
    lower_bounds = jnp.cumsum(jax.nn.softmax(hg_lb_logits.astype(F32), axis=0), axis=0)

    h = _prenorm_call(xs, norm_g[0, 0], mod[0, 1], mod[0, 0])
    kv = kpe = None
    for l in range(depth):
        sh_m, sc_m, g_m, sh_f, sc_f, g_f = (mod[l, i] for i in range(6))
        if l < n_a:
            w_in = hg_w_in[l]
            lb = lower_bounds[l].reshape(1, fdim)
            (q,) = _mm_call(h, w_in[:, :fdim].astype(BF16), _ep_silu, [(fdim, None, BF16)], tn=1024,
                            name="hg_q")
            lf, k = _mm_call(h, w_in[:, fdim:2 * fdim].astype(BF16), _ep_forget,
                             [(fdim, None, F32), (fdim, None, BF16)], extras=[(lb, "col")], name="hg_f")
            (v,) = _mm_call(h, w_in[:, 2 * fdim:2 * fdim + D].astype(BF16), _ep_bf16,
                            [(D, None, BF16)], tn=1024, name="hg_i")
            (gs,) = _mm_call(h, w_in[:, 2 * fdim + D:].astype(BF16), _ep_silu, [(D, None, BF16)],
                             tn=1024, name="hg_g")
            o = _hgrn_call(q, k, v, lf, gs, hg_norm_g[l])
            (y,) = _mm_call(o, hg_w_out[l].astype(BF16), _ep_f32, [(D, None, F32)], tn=1024,
                            name="hg_out")
        else:
            j = l - n_a
            q_lora = mla_w_dq.shape[2]
            (qd,) = _mm_call(h, mla_w_dq[j].astype(BF16), _ep_rms, [(q_lora, q_lora, BF16)],
                             extras=[(mla_q_norm_g[j].reshape(1, q_lora), "col")], tn=q_lora, name="mla_dq")
            w_uq = mla_w_uq[j].reshape(q_lora, mla_heads, QK_HEAD_DIM)
            w_uq = jnp.pad(w_uq, ((0, 0), (0, 0), (0, Q_HEAD_PAD - QK_HEAD_DIM)))
            w_uq = w_uq.reshape(q_lora, mla_heads * Q_HEAD_PAD).astype(BF16)
            (q,) = _mm_call(qd, w_uq, _ep_q_rope, [(w_uq.shape[1], None, BF16)],
                            extras=[(cos_t, "row"), (sin_t, "row")], tn=2048, name="mla_uq")
            o = _attn_call(q, kv, kpe)
            (y,) = _mm_call(o, mla_w_o[j].astype(BF16), _ep_f32, [(D, None, F32)], name="mla_o")
        xs, h = _resid_call(xs, y, g_m, norm_g[l, 1], [(norm_g[l, 2], sc_f, sh_f)])
        y = _swiglu(h, ffn_w_gate[l], ffn_w_up[l], ffn_w_down[l])
        norms = []
        if l + 1 < depth:
            norms.append((norm_g[l + 1, 0], mod[l + 1, 1], mod[l + 1, 0]))
        if l == n_a - 1:
            norms.append((kv_norm_in_g,))
        res = _resid_call(xs, y, g_f, norm_g[l, 3], norms)
        xs = res[0]
        if l + 1 < depth:
            h = res[1]
        if l == n_a - 1:
            src = res[-1]
            n_dkv = KV_LORA + LANES
            ckvn, kpe = _mm_call(src, _pad_cols(kv_w_dkv, n_dkv).astype(BF16), _ep_dkv,
                                 [(KV_LORA, KV_LORA, BF16), (LANES, LANES, BF16)],
                                 extras=[(kv_norm_g.reshape(1, KV_LORA), "full"), (cos_t, "row"), (sin_t, "row")],
                                 tn=n_dkv, name="mla_dkv")
            (kv,) = _mm_call(ckvn, kv_w_ukv.astype(BF16), _ep_bf16, [(kv_w_ukv.shape[1], None, BF16)],
                             tm=2048, tn=1024, name="mla_ukv")
    return xs.reshape(B, S, D)
```

```python
import functools
import math

import jax
import jax.numpy as jnp
from jax import lax
from jax.experimental import pallas as pl
from jax.experimental.pallas import tpu as pltpu

F32 = jnp.float32
BF16 = jnp.bfloat16

HG_HEAD_DIM = 128
NOPE_DIM = 128
ROPE_DIM = 64
V_HEAD_DIM = 128
QK_HEAD_DIM = NOPE_DIM + ROPE_DIM
KV_LORA = 512
ROPE_THETA = 10000.0
NORM_EPS = 1e-6
ATTN_SCALE = 1.0 / math.sqrt(QK_HEAD_DIM)
Q_SCALE_LOG2 = ATTN_SCALE * math.log2(math.e)

LANES = 128
SUBLANES = 8
VMEM_BYTES_V7X = 64 * 1024 * 1024
VMEM_LIMIT_CAP = VMEM_BYTES_V7X - 8 * 1024 * 1024
VMEM_LIMIT_FLOOR = 32 * 1024 * 1024

Q_HEAD_PAD = 2 * LANES
HG_CHUNK = 128
HG_LEVELS = 7


def _vmem_limit(block_bytes, temp_bytes=0):
    est = 2 * sum(block_bytes) + temp_bytes + (4 << 20)
    return int(min(max(est, VMEM_LIMIT_FLOOR), VMEM_LIMIT_CAP))


def _nbytes(shape, dtype):
    return math.prod(shape) * jnp.dtype(dtype).itemsize


def _tile(dim, pref):
    if dim <= pref:
        return dim
    t = (pref // LANES) * LANES
    while t >= LANES:
        if dim % t == 0:
            return t
        t -= LANES
    return dim


def _sigmoid(x):
    return 1.0 / (1.0 + jnp.exp(-x))


def _silu(x):
    return x * _sigmoid(x)


def _rms(x, g):
    ms = jnp.mean(x * x, axis=-1, keepdims=True)
    return x * lax.rsqrt(ms + NORM_EPS) * g


def _ada_kernel(c_ref, w_ref, b_ref, o_ref, acc_ref, *, nk):
    k = pl.program_id(2)
    tk, tn = w_ref.shape[1], w_ref.shape[2]
    p = w_ref[0] * _silu(c_ref[...])
    part = p.reshape(tk // SUBLANES, SUBLANES, tn).sum(axis=0)

    @pl.when(k == 0)
    def _():
        acc_ref[...] = part

    @pl.when(k > 0)
    def _():
        acc_ref[...] += part

    @pl.when(k == nk - 1)
    def _():
        o_ref[0] = acc_ref[...].sum(axis=0, keepdims=True) + b_ref[0]


def _ada_call(c, ada_w, ada_b):
    L, D, N = ada_w.shape
    tk, tn = _tile(D, 1024), _tile(N, 2048)
    nk = D // tk
    c_col = c.reshape(D, 1)
    b3 = ada_b.reshape(L, 1, N)
    return pl.pallas_call(
        functools.partial(_ada_kernel, nk=nk),
        grid=(L, N // tn, nk),
        in_specs=[
            pl.BlockSpec((tk, 1), lambda l, j, k: (k, 0)),
            pl.BlockSpec((1, tk, tn), lambda l, j, k: (l, k, j)),
            pl.BlockSpec((1, 1, tn), lambda l, j, k: (l, 0, j)),
        ],
        out_specs=pl.BlockSpec((1, 1, tn), lambda l, j, k: (l, 0, j)),
        out_shape=jax.ShapeDtypeStruct((L, 1, N), F32),
        scratch_shapes=[pltpu.VMEM((SUBLANES, tn), F32)],
        compiler_params=pltpu.CompilerParams(
            dimension_semantics=("arbitrary", "arbitrary", "arbitrary"),
            vmem_limit_bytes=_vmem_limit([_nbytes((tk, tn), F32), _nbytes((tk, LANES), F32)],
                                         _nbytes((tk, tn), F32))),
        name="ada_gemv",
    )(c_col, ada_w, b3)


def _rope_kernel(pos_ref, inv_ref, cmask_ref, sgn_ref, cos_ref, sin_ref):
    ang = pos_ref[...].astype(F32) * inv_ref[...]
    cos_ref[...] = jnp.cos(ang) * cmask_ref[...]
    sin_ref[...] = jnp.sin(ang) * sgn_ref[...]


def _rope_call(positions):
    S = positions.shape[-1]
    half = ROPE_DIM // 2
    inv_freq = 1.0 / (ROPE_THETA ** (jnp.arange(0, ROPE_DIM, 2, dtype=F32) / ROPE_DIM))
    zeros = jnp.zeros((LANES - ROPE_DIM,), F32)
    inv = jnp.concatenate([inv_freq, inv_freq, zeros]).reshape(1, LANES)
    cmask = jnp.concatenate([jnp.ones((ROPE_DIM,), F32), zeros]).reshape(1, LANES)
    sgn = jnp.concatenate([-jnp.ones((half,), F32), jnp.ones((half,), F32), zeros]).reshape(1, LANES)
    tm = _tile(S, 1024)
    row = pl.BlockSpec((1, LANES), lambda i: (0, 0))
    out = pl.BlockSpec((tm, LANES), lambda i: (i, 0))
    return pl.pallas_call(
        _rope_kernel,
        grid=(S // tm,),
        in_specs=[pl.BlockSpec((tm, 1), lambda i: (i, 0)), row, row, row],
        out_specs=[out, out],
        out_shape=[jax.ShapeDtypeStruct((S, LANES), F32)] * 2,
        compiler_params=pltpu.CompilerParams(dimension_semantics=("arbitrary",)),
        name="rope_tables",
    )(positions.reshape(S, 1), inv, cmask, sgn)


def _rope_block(blk, cos_t, sin_t):
    half = ROPE_DIM // 2
    lane = lax.broadcasted_iota(jnp.int32, blk.shape, 1)
    swapped = jnp.where(lane < half, pltpu.roll(blk, LANES - half, 1), pltpu.roll(blk, half, 1))
    return blk * cos_t + swapped * sin_t


def _prenorm_kernel(x_ref, g_ref, sc_ref, sh_ref, o_ref):
    y = _rms(x_ref[...], g_ref[...])
    o_ref[...] = (y * (1.0 + sc_ref[...]) + sh_ref[...]).astype(o_ref.dtype)


def _prenorm_call(x, g, sc, sh):
    S, D = x.shape
    tm = _tile(S, 256)
    vec = pl.BlockSpec((1, D), lambda i: (0, 0))
    row = pl.BlockSpec((tm, D), lambda i: (i, 0))
    return pl.pallas_call(
        _prenorm_kernel,
        grid=(S // tm,),
        in_specs=[row, vec, vec, vec],
        out_specs=row,
        out_shape=jax.ShapeDtypeStruct((S, D), BF16),
        compiler_params=pltpu.CompilerParams(
            dimension_semantics=("arbitrary",),
            vmem_limit_bytes=_vmem_limit([_nbytes((tm, D), F32), _nbytes((tm, D), BF16)],
                                         3 * _nbytes((tm, D), F32))),
        name="prenorm",
    )(x, g.reshape(1, D), sc.reshape(1, D), sh.reshape(1, D))


def _resid_kernel(*refs, modulated):
    x_ref, y_ref, gate_ref, w_ref = refs[:4]
    n_out = len(modulated)
    n_par = sum(3 if m else 1 for m in modulated)
    par = refs[4:4 + n_par]
    xo_ref = refs[4 + n_par]
    outs = refs[5 + n_par:5 + n_par + n_out]
    xn = x_ref[...] + gate_ref[...] * _rms(y_ref[...], w_ref[...])
    xo_ref[...] = xn
    if n_out:
        inv = lax.rsqrt(jnp.mean(xn * xn, axis=-1, keepdims=True) + NORM_EPS)
        xh = xn * inv
        p = 0
        for m, o_ref in zip(modulated, outs):
            h = xh * par[p][...]
            if m:
                h = h * (1.0 + par[p + 1][...]) + par[p + 2][...]
            p += 3 if m else 1
            o_ref[...] = h.astype(o_ref.dtype)


def _resid_call(x, y, gate, w, norms):
    S, D = x.shape
    tm = _tile(S, 256)
    vec = pl.BlockSpec((1, D), lambda i: (0, 0))
    row = pl.BlockSpec((tm, D), lambda i: (i, 0))
    modulated = tuple(len(n) == 3 for n in norms)
    params = [p.reshape(1, D) for n in norms for p in n]
    n_out = len(norms)
    return pl.pallas_call(
        functools.partial(_resid_kernel, modulated=modulated),
        grid=(S // tm,),
        in_specs=[row, row, vec, vec] + [vec] * len(params),
        out_specs=[row] * (1 + n_out),
        out_shape=[jax.ShapeDtypeStruct((S, D), F32)] + [jax.ShapeDtypeStruct((S, D), BF16)] * n_out,
        compiler_params=pltpu.CompilerParams(
            dimension_semantics=("arbitrary",),
            vmem_limit_bytes=_vmem_limit([_nbytes((tm, D), F32)] * 3 + [_nbytes((tm, D), BF16)] * n_out,
                                         4 * _nbytes((tm, D), F32))),
        name="resid_norm",
    )(x, y, gate.reshape(1, D), w.reshape(1, D), *params)


def _mm_kernel(x_ref, w_ref, *rest, nk, n_extra, n_out, epilogue):
    extras = rest[:n_extra]
    outs = rest[n_extra:n_extra + n_out]
    w = w_ref[0] if len(w_ref.shape) == 3 else w_ref[...]
    prod = jnp.dot(x_ref[...], w, preferred_element_type=F32)
    if nk == 1:
        epilogue(prod, extras, outs)
        return
    acc_ref = rest[n_extra + n_out]
    k = pl.program_id(2)

    @pl.when(k == 0)
    def _():
        acc_ref[...] = prod

    @pl.when(k > 0)
    def _():
        acc_ref[...] += prod

    @pl.when(k == nk - 1)
    def _():
        epilogue(acc_ref[...], extras, outs)


def _mm_call(x, w, epilogue, outs, *, extras=(), layer=None, tm=1024, tn=512, tk=4096, name="matmul"):
    M, K = x.shape
    N = w.shape[-1]
    tm, tn, tk = _tile(M, tm), _tile(N, tn), _tile(K, tk)
    nk = K // tk
    if layer is None:
        w_spec = pl.BlockSpec((tk, tn), lambda i, j, k: (k, j))
    else:
        w_spec = pl.BlockSpec((1, tk, tn), lambda i, j, k: (layer, k, j))
    in_specs = [pl.BlockSpec((tm, tk), lambda i, j, k: (i, k)), w_spec]
    blocks = [_nbytes((tm, tk), x.dtype), _nbytes((tk, tn), w.dtype)]
    arrays = []
    for arr, kind in extras:
        if kind == "col":
            in_specs.append(pl.BlockSpec((1, tn), lambda i, j, k: (0, j)))
            blocks.append(_nbytes((SUBLANES, tn), arr.dtype))
        elif kind == "row":
            in_specs.append(pl.BlockSpec((tm, arr.shape[1]), lambda i, j, k: (i, 0)))
            blocks.append(_nbytes((tm, arr.shape[1]), arr.dtype))
        else:
            in_specs.append(pl.BlockSpec(arr.shape, lambda i, j, k: (0,) * arr.ndim))
            blocks.append(_nbytes(arr.shape, arr.dtype))
        arrays.append(arr)
    out_specs, out_shape = [], []
    for n_tot, n_blk, dt in outs:
        n_blk = tn if n_blk is None else n_blk
        out_specs.append(pl.BlockSpec((tm, n_blk), lambda i, j, k: (i, j)))
        out_shape.append(jax.ShapeDtypeStruct((M, n_tot), dt))
        blocks.append(_nbytes((tm, n_blk), dt))
    scratch = [pltpu.VMEM((tm, tn), F32)] if nk > 1 else []
    return pl.pallas_call(
        functools.partial(_mm_kernel, nk=nk, n_extra=len(arrays), n_out=len(outs), epilogue=epilogue),
        grid=(M // tm, N // tn, nk),
        in_specs=in_specs,
        out_specs=out_specs,
        out_shape=out_shape,
        scratch_shapes=scratch,
        compiler_params=pltpu.CompilerParams(
            dimension_semantics=("arbitrary", "arbitrary", "arbitrary"),
            vmem_limit_bytes=_vmem_limit(blocks, 2 * _nbytes((tm, tn), F32))),
        name=name,
    )(x, w, *arrays)


def _mm_ws_kernel(x_ref, *rest, n_w, n_extra, n_out, epilogue):
    w_refs = rest[:n_w]
    extras = rest[n_w:n_w + n_extra]
    outs = rest[n_w + n_extra:n_w + n_extra + n_out]
    wbf = rest[n_w + n_extra + n_out:]

    @pl.when(pl.program_id(1) == 0)
    def _():
        for w_ref, wb in zip(w_refs, wbf):
            wb[...] = w_ref[0].astype(BF16)

    x = x_ref[...]
    prods = [jnp.dot(x, wb[...], preferred_element_type=F32) for wb in wbf]
    epilogue(prods[0] if n_w == 1 else prods, extras, outs)


def _mm_ws_call(x, ws, layer, col0, n_cols, epilogue, outs, *, extras=(), tm=1024, tn=512, name="matmul_ws"):
    M, K = x.shape
    tm, tn = _tile(M, tm), _tile(n_cols, tn)
    assert col0 % tn == 0
    off = col0 // tn
    in_specs = [pl.BlockSpec((tm, K), lambda j, i: (i, 0))]
    in_specs += [pl.BlockSpec((1, K, tn), lambda j, i: (layer, 0, j + off))] * len(ws)
    blocks = [_nbytes((tm, K), x.dtype)] + [_nbytes((K, tn), F32)] * len(ws)
    arrays = []
    for arr, kind in extras:
        if kind == "col":
            in_specs.append(pl.BlockSpec((1, tn), lambda j, i: (0, j)))
            blocks.append(_nbytes((SUBLANES, tn), arr.dtype))
        elif kind == "row":
            in_specs.append(pl.BlockSpec((tm, arr.shape[1]), lambda j, i: (i, 0)))
            blocks.append(_nbytes((tm, arr.shape[1]), arr.dtype))
        else:
            in_specs.append(pl.BlockSpec(arr.shape, lambda j, i: (0,) * arr.ndim))
            blocks.append(_nbytes(arr.shape, arr.dtype))
        arrays.append(arr)
    out_specs, out_shape = [], []
    for n_tot, n_blk, dt in outs:
        n_blk = tn if n_blk is None else n_blk
        out_specs.append(pl.BlockSpec((tm, n_blk), lambda j, i: (i, j)))
        out_shape.append(jax.ShapeDtypeStruct((M, n_tot), dt))
        blocks.append(_nbytes((tm, n_blk), dt))
    scratch_bytes = len(ws) * _nbytes((K, tn), BF16)
    return pl.pallas_call(
        functools.partial(_mm_ws_kernel, n_w=len(ws), n_extra=len(arrays), n_out=len(outs), epilogue=epilogue),
        grid=(n_cols // tn, M // tm),
        in_specs=in_specs,
        out_specs=out_specs,
        out_shape=out_shape,
        scratch_shapes=[pltpu.VMEM((K, tn), BF16)] * len(ws),
        compiler_params=pltpu.CompilerParams(
            dimension_semantics=("arbitrary", "arbitrary"),
            vmem_limit_bytes=_vmem_limit(blocks, scratch_bytes + (len(ws) + 1) * _nbytes((tm, tn), F32))),
        name=name,
    )(x, *ws, *arrays)


def _ep_swiglu(acc, extras, outs):
    g, u = acc
    outs[0][...] = (_silu(g) * u).astype(BF16)


def _ep_f32(acc, extras, outs):
    outs[0][...] = acc


def _ep_bf16(acc, extras, outs):
    outs[0][...] = acc.astype(BF16)


def _ep_silu(acc, extras, outs):
    outs[0][...] = _silu(acc).astype(BF16)


def _ep_forget(acc, extras, outs):
    lb = extras[0][...]
    f = lb + (1.0 - lb) * _sigmoid(acc)
    outs[0][...] = jnp.log2(f)
    outs[1][...] = (1.0 - f).astype(BF16)


def _ep_rms(acc, extras, outs):
    outs[0][...] = _rms(acc, extras[0][...]).astype(BF16)


def _ep_q_rope(acc, extras, outs):
    cos_t, sin_t = extras[0][...], extras[1][...]
    for h in range(acc.shape[1] // Q_HEAD_PAD):
        lo = h * Q_HEAD_PAD
        outs[0][:, lo:lo + LANES] = (acc[:, lo:lo + LANES] * Q_SCALE_LOG2).astype(BF16)
        pe = _rope_block(acc[:, lo + LANES:lo + Q_HEAD_PAD], cos_t, sin_t)
        outs[0][:, lo + LANES:lo + Q_HEAD_PAD] = (pe * Q_SCALE_LOG2).astype(BF16)


def _ep_dkv(acc, extras, outs):
    g, cos_t, sin_t = extras[0][...], extras[1][...], extras[2][...]
    outs[0][...] = _rms(acc[:, :KV_LORA], g).astype(BF16)
    outs[1][...] = _rope_block(acc[:, KV_LORA:KV_LORA + LANES], cos_t, sin_t).astype(BF16)


def _boundary_cumsum(b, m):
    T, W = b.shape
    if 2 * m >= SUBLANES:
        nb = T // (2 * m)
        b3 = b.reshape(nb, 2 * m, W)
        return jnp.broadcast_to(b3[:, m - 1:m, :], (nb, 2 * m, W)).reshape(T, W)
    b3 = b.reshape(T // SUBLANES, SUBLANES, W)
    sub = lax.broadcasted_iota(jnp.int32, b3.shape, 1)
    r = jnp.broadcast_to(b3[:, m - 1:m, :], b3.shape)
    for blk in range(1, SUBLANES // (2 * m)):
        row = blk * 2 * m + m - 1
        r = jnp.where(sub >= blk * 2 * m, jnp.broadcast_to(b3[:, row:row + 1, :], b3.shape), r)
    return r.reshape(T, W)


def _hgrn_kernel(q_ref, k_ref, v_ref, lf_ref, gs_ref, ng_ref, tril_ref, sgn_ref, rowsel_ref, pmask_ref,
                 o_ref, st_ref, *, n_chunks, nh):
    T = HG_CHUNK

    @pl.when(pl.program_id(1) == 0)
    def _():
        st_ref[...] = jnp.zeros_like(st_ref)

    nt = (((1,), (1,)), ((), ()))
    tn = (((0,), (0,)), ((), ()))
    heads = range(nh)
    cols = [slice(hh * HG_HEAD_DIM, (hh + 1) * HG_HEAD_DIM) for hh in heads]

    def chunk(c, carry):
        rows = pl.ds(pl.multiple_of(c * T, T), T)
        lf = [lf_ref[rows, cs] for cs in cols]
        b = []
        for hh in heads:
            hi = lf[hh].astype(BF16)
            r1 = lf[hh] - hi.astype(F32)
            mid = r1.astype(BF16)
            lo = (r1 - mid.astype(F32)).astype(BF16)
            parts = jnp.dot(tril_ref[...], jnp.concatenate([hi, mid, lo], axis=1),
                            preferred_element_type=F32)
            b.append(parts[:, :LANES] + parts[:, LANES:2 * LANES] + parts[:, 2 * LANES:])
        qb = [q_ref[rows, cs] for cs in cols]
        kb = [k_ref[rows, cs] for cs in cols]
        vb = [v_ref[rows, cs] for cs in cols]
        q = [t.astype(F32) for t in qb]
        k = [t.astype(F32) for t in kb]
        st = [st_ref[hh] for hh in heads]
        o = [lax.dot_general((q[hh] * jnp.exp2(b[hh])).astype(BF16), st[hh].astype(BF16), nt,
                             preferred_element_type=F32) for hh in heads]
        a = [jnp.where(pmask_ref[HG_LEVELS] > 0.5,
                       lax.dot_general(qb[hh], kb[hh], nt, preferred_element_type=F32), 0.0)
             for hh in heads]
        for lvl in range(HG_LEVELS):
            second = rowsel_ref[lvl] > 0.5
            for hh in heads:
                if lvl == 0:
                    arg = lf[hh] * rowsel_ref[0]
                else:
                    arg = (b[hh] - _boundary_cumsum(b[hh], 1 << lvl)) * sgn_ref[lvl]
                x = (jnp.where(second, q[hh], k[hh]) * jnp.exp2(arg)).astype(BF16)
                p = lax.dot_general(x, x, nt, preferred_element_type=F32)
                a[hh] = jnp.where(pmask_ref[lvl] > 0.5, p, a[hh])
        for hh in heads:
            o[hh] = o[hh] + jnp.dot(a[hh].astype(BF16), vb[hh], preferred_element_type=F32)
        for hh in heads:
            b_last = b[hh][T - 1:T, :]
            kt = (k[hh] * jnp.exp2(b_last - b[hh])).astype(BF16)
            st_ref[hh] = st[hh] * jnp.exp2(b_last) + lax.dot_general(vb[hh], kt, tn,
                                                                     preferred_element_type=F32)
        for hh in heads:
            y = _rms(o[hh], ng_ref[...]) * gs_ref[rows, cols[hh]].astype(F32)
            o_ref[rows, cols[hh]] = y.astype(o_ref.dtype)
        return carry

    lax.fori_loop(0, n_chunks, chunk, 0)


def _hgrn_masks():
    T = HG_CHUNK
    t = jnp.arange(T)
    tril = (t[:, None] >= t[None, :]).astype(BF16)
    rowsel, pmask = [], []
    for lvl in range(HG_LEVELS):
        m = 1 << lvl
        second = (t // m) % 2 == 1
        rowsel.append(jnp.broadcast_to(second[:, None], (T, HG_HEAD_DIM)))
        same = (t[:, None] // (2 * m)) == (t[None, :] // (2 * m))
        pmask.append(same & second[:, None] & (~second)[None, :])
    pmask.append(t[:, None] == t[None, :])
    rowsel = jnp.stack(rowsel).astype(F32)
    return tril, 2.0 * rowsel - 1.0, rowsel, jnp.stack(pmask).astype(F32)


def _hgrn_call(q, k, v, lf2, gs, norm_g, *, t_blk=1024, nh=8):
    S, D = q.shape
    H = D // HG_HEAD_DIM
    t_blk = _tile(S, t_blk)
    nh = math.gcd(nh, H)
    assert t_blk % HG_CHUNK == 0
    tril, sgn, rowsel, pmask = _hgrn_masks()
    w_blk = nh * HG_HEAD_DIM
    seq = pl.BlockSpec((t_blk, w_blk), lambda h, s: (s, h))

    def const(a):
        return pl.BlockSpec(a.shape, lambda h, s: (0,) * a.ndim)

    ng = norm_g.reshape(1, HG_HEAD_DIM)
    return pl.pallas_call(
        functools.partial(_hgrn_kernel, n_chunks=t_blk // HG_CHUNK, nh=nh),
        grid=(H // nh, S // t_blk),
        in_specs=[seq, seq, seq, seq, seq, const(ng), const(tril), const(sgn), const(rowsel), const(pmask)],
        out_specs=seq,
        out_shape=jax.ShapeDtypeStruct((S, D), BF16),
        scratch_shapes=[pltpu.VMEM((nh, HG_HEAD_DIM, HG_HEAD_DIM), F32)],
        compiler_params=pltpu.CompilerParams(
            dimension_semantics=("arbitrary", "arbitrary"),
            vmem_limit_bytes=_vmem_limit([5 * _nbytes((t_blk, w_blk), BF16), _nbytes((t_blk, w_blk), F32)],
                                         64 * nh * _nbytes((HG_CHUNK, HG_HEAD_DIM), F32))),
        name="hgrn2_recurrence",
    )(q, k, v, lf2, gs, ng, tril, sgn, rowsel, pmask)


def _attn_kernel(q_ref, kn_ref, v_ref, kpe_ref, o_ref, m_ref, acc_ref, *, tq, tk, nsub):
    qi = pl.program_id(1)
    tb = tq * nsub
    kv_per_block = tb // tk
    nt = (((1,), (1,)), ((), ()))
    m_ref[...] = jnp.full_like(m_ref, -jnp.inf)
    acc_ref[...] = jnp.zeros_like(acc_ref)
    ones = jnp.ones((tk, LANES), BF16)

    def tile_update(a, kc, ve, mask_off):
        q = q_ref[a * tq:(a + 1) * tq, :]
        s = lax.dot_general(q, kc, nt, preferred_element_type=F32)
        if mask_off is not None:
            r = lax.broadcasted_iota(jnp.int32, s.shape, 0)
            c = lax.broadcasted_iota(jnp.int32, s.shape, 1)
            s = jnp.where(r + mask_off >= c, s, -jnp.inf)
        m_old = m_ref[a]
        m_new = jnp.maximum(m_old, s.max(axis=-1, keepdims=True))
        alpha = jnp.exp2(m_old - m_new)
        p = jnp.concatenate([jnp.exp2(s[:, cb * LANES:(cb + 1) * LANES] - m_new)
                             for cb in range(tk // LANES)], axis=1).astype(BF16)
        pv = jnp.dot(p, ve, preferred_element_type=F32)
        acc_ref[a] = jnp.concatenate([alpha, alpha], axis=1) * acc_ref[a] + pv
        m_ref[a] = m_new

    def load_kv(j):
        rows = pl.ds(pl.multiple_of(j * tk, tk), tk)
        kc = jnp.concatenate([kn_ref[rows, :], kpe_ref[rows, :]], axis=1)
        ve = jnp.concatenate([v_ref[rows, :], ones], axis=1)
        return kc, ve

    def body(j, carry):
        kc, ve = load_kv(j)
        for a in range(nsub):
            tile_update(a, kc, ve, None)
        return carry

    lax.fori_loop(0, qi * kv_per_block, body, 0)
    for b in range(kv_per_block):
        kc, ve = load_kv(qi * kv_per_block + b)
        for a in range(nsub):
            if b * tk > (a + 1) * tq - 1:
                continue
            unmasked = (b + 1) * tk - 1 <= a * tq
            tile_update(a, kc, ve, None if unmasked else a * tq - b * tk)
    for a in range(nsub):
        acc = acc_ref[a]
        o_ref[a * tq:(a + 1) * tq, :] = (acc[:, :V_HEAD_DIM] / acc[:, V_HEAD_DIM:]).astype(o_ref.dtype)


def _attn_call(q, kv, kpe, *, tq=256, tk=1024, nsub=8):
    S = q.shape[0]
    H = q.shape[1] // Q_HEAD_PAD
    tk = _tile(S, tk)
    nsub = min(nsub, S // tq)
    tb = tq * nsub
    assert S % tb == 0 and tb % tk == 0
    return pl.pallas_call(
        functools.partial(_attn_kernel, tq=tq, tk=tk, nsub=nsub),
        grid=(H, S // tb),
        in_specs=[
            pl.BlockSpec((tb, Q_HEAD_PAD), lambda h, i: (i, h)),
            pl.BlockSpec((S, NOPE_DIM), lambda h, i: (0, 2 * h)),
            pl.BlockSpec((S, V_HEAD_DIM), lambda h, i: (0, 2 * h + 1)),
            pl.BlockSpec((S, LANES), lambda h, i: (0, 0)),
        ],
        out_specs=pl.BlockSpec((tb, V_HEAD_DIM), lambda h, i: (i, h)),
        out_shape=jax.ShapeDtypeStruct((S, H * V_HEAD_DIM), BF16),
        scratch_shapes=[pltpu.VMEM((nsub, tq, LANES), F32), pltpu.VMEM((nsub, tq, 2 * LANES), F32)],
        compiler_params=pltpu.CompilerParams(
            dimension_semantics=("arbitrary", "arbitrary"),
            vmem_limit_bytes=_vmem_limit([3 * _nbytes((S, LANES), BF16), _nbytes((tb, Q_HEAD_PAD), BF16),
                                          _nbytes((tb, V_HEAD_DIM), BF16)],
                                         3 * nsub * _nbytes((tq, tk), F32))),
        name="mla_attention",
    )(q, kv, kv, kpe)


def _pad_cols(w, n):
    return jnp.pad(w, ((0, 0), (0, n - w.shape[1])))


def _swiglu(h, w_gate, w_up, w_down_bf16, layer):
    F = w_gate.shape[2]
    (a,) = _mm_ws_call(h, [w_gate, w_up], layer, 0, F, _ep_swiglu, [(F, None, BF16)], tn=256, name="ffn_up")
    (y,) = _mm_call(a, w_down_bf16, _ep_f32, [(w_down_bf16.shape[2], None, F32)], layer=layer,
                    tk=F // 2 if F % (2 * LANES) == 0 else F, name="ffn_down")
    return y


def kernel(x, c, positions, ada_w, ada_b, norm_g, ffn_w_gate, ffn_w_up, ffn_w_down, hg_w_in, hg_lb_logits, hg_norm_g, hg_w_out, mla_w_dq, mla_q_norm_g, mla_w_uq, mla_w_o, kv_norm_in_g, kv_w_dkv, kv_norm_g, kv_w_ukv):
    B, S, D = x.shape
    assert B == 1
    depth = ada_w.shape[0]
    n_a = hg_w_in.shape[0]
    mla_heads = kv_w_ukv.shape[1] // (NOPE_DIM + V_HEAD_DIM)
    fdim = hg_lb_logits.shape[1]

    xs = x.reshape(S, D)
    mod = _ada_call(c, ada_w, ada_b).reshape(depth, 6, D)
    cos_t, sin_t = _rope_call(positions)
    lower_bounds = jnp.cumsum(jax.nn.softmax(hg_lb_logits.astype(F32), axis=0), axis=0)

    w_down_bf16 = ffn_w_down.astype(BF16)
    w_o_bf16 = mla_w_o.astype(BF16)

    h = _prenorm_call(xs, norm_g[0, 0], mod[0, 1], mod[0, 0])
    kv = kpe = None
    for l in range(depth):
        sh_m, sc_m, g_m, sh_f, sc_f, g_f = (mod[l, i] for i in range(6))
        if l < n_a:
            lb = lower_bounds[l].reshape(1, fdim)
            w_in = [hg_w_in]
            (q,) = _mm_ws_call(h, w_in, l, 0, fdim, _ep_silu, [(fdim, None, BF16)], name="hg_q")
            lf, k = _mm_ws_call(h, w_in, l, fdim, fdim, _ep_forget, [(fdim, None, F32), (fdim, None, BF16)],
                                extras=[(lb, "col")], name="hg_f")
            (v,) = _mm_ws_call(h, w_in, l, 2 * fdim, D, _ep_bf16, [(D, None, BF16)], name="hg_i")
            (gs,) = _mm_ws_call(h, w_in, l, 2 * fdim + D, D, _ep_silu, [(D, None, BF16)], name="hg_g")
            o = _hgrn_call(q, k, v, lf, gs, hg_norm_g[l])
            (y,) = _mm_ws_call(o, [hg_w_out], l, 0, D, _ep_f32, [(D, None, F32)], name="hg_out")
        else:
            j = l - n_a
            q_lora = mla_w_dq.shape[2]
            (qd,) = _mm_call(h, mla_w_dq[j].astype(BF16), _ep_rms, [(q_lora, q_lora, BF16)],
                             extras=[(mla_q_norm_g[j].reshape(1, q_lora), "col")], tn=q_lora, name="mla_dq")
            w_uq = mla_w_uq[j].reshape(q_lora, mla_heads, QK_HEAD_DIM)
            w_uq = jnp.pad(w_uq, ((0, 0), (0, 0), (0, Q_HEAD_PAD - QK_HEAD_DIM)))
            w_uq = w_uq.reshape(q_lora, mla_heads * Q_HEAD_PAD).astype(BF16)
            (q,) = _mm_call(qd, w_uq, _ep_q_rope, [(w_uq.shape[1], None, BF16)],
                            extras=[(cos_t, "row"), (sin_t, "row")], tn=2048, name="mla_uq")
            o = _attn_call(q, kv, kpe)
            (y,) = _mm_call(o, w_o_bf16, _ep_f32, [(D, None, F32)], layer=j, name="mla_o")
        xs, h = _resid_call(xs, y, g_m, norm_g[l, 1], [(norm_g[l, 2], sc_f, sh_f)])
        y = _swiglu(h, ffn_w_gate, ffn_w_up, w_down_bf16, l)
        norms = []
        if l + 1 < depth:
            norms.append((norm_g[l + 1, 0], mod[l + 1, 1], mod[l + 1, 0]))
        if l == n_a - 1:
            norms.append((kv_norm_in_g,))
        res = _resid_call(xs, y, g_f, norm_g[l, 3], norms)
        xs = res[0]
        if l + 1 < depth:
            h = res[1]
        if l == n_a - 1:
            src = res[-1]
            n_dkv = KV_LORA + LANES
            ckvn, kpe = _mm_call(src, _pad_cols(kv_w_dkv, n_dkv).astype(BF16), _ep_dkv,
                                 [(KV_LORA, KV_LORA, BF16), (LANES, LANES, BF16)],
                                 extras=[(kv_norm_g.reshape(1, KV_LORA), "full"), (cos_t, "row"), (sin_t, "row")],
                                 tn=n_dkv, name="mla_dkv")
            (kv,) = _mm_call(ckvn, kv_w_ukv.astype(BF16), _ep_bf16, [(kv_w_ukv.shape[1], None, BF16)],
                             tm=2048, tn=1024, name="mla_ukv")
    return xs.reshape(B, S, D)
```

```python
import functools
import math

import jax
import jax.numpy as jnp
from jax import lax
from jax.experimental import pallas as pl
from jax.experimental.pallas import tpu as pltpu

F32 = jnp.float32
BF16 = jnp.bfloat16

HG_HEAD_DIM = 128
NOPE_DIM = 128
ROPE_DIM = 64
V_HEAD_DIM = 128
QK_HEAD_DIM = NOPE_DIM + ROPE_DIM
KV_LORA = 512
ROPE_THETA = 10000.0
NORM_EPS = 1e-6
ATTN_SCALE = 1.0 / math.sqrt(QK_HEAD_DIM)
Q_SCALE_LOG2 = ATTN_SCALE * math.log2(math.e)

LANES = 128
SUBLANES = 8
VMEM_BYTES_V7X = 64 * 1024 * 1024
VMEM_LIMIT_CAP = VMEM_BYTES_V7X - 8 * 1024 * 1024
VMEM_LIMIT_FLOOR = 32 * 1024 * 1024

Q_HEAD_PAD = 2 * LANES
HG_CHUNK = 128
HG_LEVELS = 7


def _vmem_limit(block_bytes, temp_bytes=0):
    est = 2 * sum(block_bytes) + temp_bytes + (4 << 20)
    return int(min(max(est, VMEM_LIMIT_FLOOR), VMEM_LIMIT_CAP))


def _nbytes(shape, dtype):
    return math.prod(shape) * jnp.dtype(dtype).itemsize


def _tile(dim, pref):
    if dim <= pref:
        return dim
    t = (pref // LANES) * LANES
    while t >= LANES:
        if dim % t == 0:
            return t
        t -= LANES
    return dim


def _sigmoid(x):
    return 1.0 / (1.0 + jnp.exp(-x))


def _silu(x):
    return x * _sigmoid(x)


def _rms(x, g):
    ms = jnp.mean(x * x, axis=-1, keepdims=True)
    return x * lax.rsqrt(ms + NORM_EPS) * g


def _ada_kernel(c_ref, w_ref, b_ref, o_ref, acc_ref, *, nk):
    k = pl.program_id(2)
    tk, tn = w_ref.shape[1], w_ref.shape[2]
    p = w_ref[0] * _silu(c_ref[...])
    part = p.reshape(tk // SUBLANES, SUBLANES, tn).sum(axis=0)

    @pl.when(k == 0)
    def _():
        acc_ref[...] = part

    @pl.when(k > 0)
    def _():
        acc_ref[...] += part

    @pl.when(k == nk - 1)
    def _():
        o_ref[0] = acc_ref[...].sum(axis=0, keepdims=True) + b_ref[0]


def _ada_call(c, ada_w, ada_b):
    L, D, N = ada_w.shape
    tk, tn = _tile(D, 1024), _tile(N, 2048)
    nk = D // tk
    c_col = c.reshape(D, 1)
    b3 = ada_b.reshape(L, 1, N)
    return pl.pallas_call(
        functools.partial(_ada_kernel, nk=nk),
        grid=(L, N // tn, nk),
        in_specs=[
            pl.BlockSpec((tk, 1), lambda l, j, k: (k, 0)),
            pl.BlockSpec((1, tk, tn), lambda l, j, k: (l, k, j)),
            pl.BlockSpec((1, 1, tn), lambda l, j, k: (l, 0, j)),
        ],
        out_specs=pl.BlockSpec((1, 1, tn), lambda l, j, k: (l, 0, j)),
        out_shape=jax.ShapeDtypeStruct((L, 1, N), F32),
        scratch_shapes=[pltpu.VMEM((SUBLANES, tn), F32)],
        compiler_params=pltpu.CompilerParams(
            dimension_semantics=("arbitrary", "arbitrary", "arbitrary"),
            vmem_limit_bytes=_vmem_limit([_nbytes((tk, tn), F32), _nbytes((tk, LANES), F32)],
                                         _nbytes((tk, tn), F32))),
        name="ada_gemv",
    )(c_col, ada_w, b3)


def _rope_kernel(pos_ref, inv_ref, cmask_ref, sgn_ref, cos_ref, sin_ref):
    ang = pos_ref[...].astype(F32) * inv_ref[...]
    cos_ref[...] = jnp.cos(ang) * cmask_ref[...]
    sin_ref[...] = jnp.sin(ang) * sgn_ref[...]


def _rope_call(positions):
    S = positions.shape[-1]
    half = ROPE_DIM // 2
    inv_freq = 1.0 / (ROPE_THETA ** (jnp.arange(0, ROPE_DIM, 2, dtype=F32) / ROPE_DIM))
    zeros = jnp.zeros((LANES - ROPE_DIM,), F32)
    inv = jnp.concatenate([inv_freq, inv_freq, zeros]).reshape(1, LANES)
    cmask = jnp.concatenate([jnp.ones((ROPE_DIM,), F32), zeros]).reshape(1, LANES)
    sgn = jnp.concatenate([-jnp.ones((half,), F32), jnp.ones((half,), F32), zeros]).reshape(1, LANES)
    tm = _tile(S, 1024)
    row = pl.BlockSpec((1, LANES), lambda i: (0, 0))
    out = pl.BlockSpec((tm, LANES), lambda i: (i, 0))
    return pl.pallas_call(
        _rope_kernel,
        grid=(S // tm,),
        in_specs=[pl.BlockSpec((tm, 1), lambda i: (i, 0)), row, row, row],
        out_specs=[out, out],
        out_shape=[jax.ShapeDtypeStruct((S, LANES), F32)] * 2,
        compiler_params=pltpu.CompilerParams(dimension_semantics=("arbitrary",)),
        name="rope_tables",
    )(positions.reshape(S, 1), inv, cmask, sgn)


def _rope_block(blk, cos_t, sin_t):
    half = ROPE_DIM // 2
    lane = lax.broadcasted_iota(jnp.int32, blk.shape, 1)
    swapped = jnp.where(lane < half, pltpu.roll(blk, LANES - half, 1), pltpu.roll(blk, half, 1))
    return blk * cos_t + swapped * sin_t


def _prenorm_kernel(x_ref, g_ref, sc_ref, sh_ref, o_ref):
    y = _rms(x_ref[...], g_ref[...])
    o_ref[...] = (y * (1.0 + sc_ref[...]) + sh_ref[...]).astype(o_ref.dtype)


def _prenorm_call(x, g, sc, sh):
    S, D = x.shape
    tm = _tile(S, 256)
    vec = pl.BlockSpec((1, D), lambda i: (0, 0))
    row = pl.BlockSpec((tm, D), lambda i: (i, 0))
    return pl.pallas_call(
        _prenorm_kernel,
        grid=(S // tm,),
        in_specs=[row, vec, vec, vec],
        out_specs=row,
        out_shape=jax.ShapeDtypeStruct((S, D), BF16),
        compiler_params=pltpu.CompilerParams(
            dimension_semantics=("arbitrary",),
            vmem_limit_bytes=_vmem_limit([_nbytes((tm, D), F32), _nbytes((tm, D), BF16)],
                                         3 * _nbytes((tm, D), F32))),
        name="prenorm",
    )(x, g.reshape(1, D), sc.reshape(1, D), sh.reshape(1, D))


def _resid_kernel(*refs, modulated):
    x_ref, y_ref, gate_ref, w_ref = refs[:4]
    n_out = len(modulated)
    n_par = sum(3 if m else 1 for m in modulated)
    par = refs[4:4 + n_par]
    xo_ref = refs[4 + n_par]
    outs = refs[5 + n_par:5 + n_par + n_out]
    xn = x_ref[...] + gate_ref[...] * _rms(y_ref[...], w_ref[...])
    xo_ref[...] = xn
    if n_out:
        inv = lax.rsqrt(jnp.mean(xn * xn, axis=-1, keepdims=True) + NORM_EPS)
        xh = xn * inv
        p = 0
        for m, o_ref in zip(modulated, outs):
            h = xh * par[p][...]
            if m:
                h = h * (1.0 + par[p + 1][...]) + par[p + 2][...]
            p += 3 if m else 1
            o_ref[...] = h.astype(o_ref.dtype)


def _resid_call(x, y, gate, w, norms):
    S, D = x.shape
    tm = _tile(S, 256)
    vec = pl.BlockSpec((1, D), lambda i: (0, 0))
    row = pl.BlockSpec((tm, D), lambda i: (i, 0))
    modulated = tuple(len(n) == 3 for n in norms)
    params = [p.reshape(1, D) for n in norms for p in n]
    n_out = len(norms)
    return pl.pallas_call(
        functools.partial(_resid_kernel, modulated=modulated),
        grid=(S // tm,),
        in_specs=[row, row, vec, vec] + [vec] * len(params),
        out_specs=[row] * (1 + n_out),
        out_shape=[jax.ShapeDtypeStruct((S, D), F32)] + [jax.ShapeDtypeStruct((S, D), BF16)] * n_out,
        compiler_params=pltpu.CompilerParams(
            dimension_semantics=("arbitrary",),
            vmem_limit_bytes=_vmem_limit([_nbytes((tm, D), F32)] * 3 + [_nbytes((tm, D), BF16)] * n_out,
                                         4 * _nbytes((tm, D), F32))),
        name="resid_norm",
    )(x, y, gate.reshape(1, D), w.reshape(1, D), *params)


def _mm_kernel(x_ref, w_ref, *rest, nk, n_extra, n_out, epilogue):
    extras = rest[:n_extra]
    outs = rest[n_extra:n_extra + n_out]
    w = w_ref[0] if len(w_ref.shape) == 3 else w_ref[...]
    prod = jnp.dot(x_ref[...], w, preferred_element_type=F32)
    if nk == 1:
        epilogue(prod, extras, outs)
        return
    acc_ref = rest[n_extra + n_out]
    k = pl.program_id(2)

    @pl.when(k == 0)
    def _():
        acc_ref[...] = prod

    @pl.when(k > 0)
    def _():
        acc_ref[...] += prod

    @pl.when(k == nk - 1)
    def _():
        epilogue(acc_ref[...], extras, outs)


def _mm_call(x, w, epilogue, outs, *, extras=(), layer=None, tm=1024, tn=512, tk=4096, name="matmul"):
    M, K = x.shape
    N = w.shape[-1]
    tm, tn, tk = _tile(M, tm), _tile(N, tn), _tile(K, tk)
    nk = K // tk
    if layer is None:
        w_spec = pl.BlockSpec((tk, tn), lambda i, j, k: (k, j))
    else:
        w_spec = pl.BlockSpec((1, tk, tn), lambda i, j, k: (layer, k, j))
    in_specs = [pl.BlockSpec((tm, tk), lambda i, j, k: (i, k)), w_spec]
    blocks = [_nbytes((tm, tk), x.dtype), _nbytes((tk, tn), w.dtype)]
    arrays = []
    for arr, kind in extras:
        if kind == "col":
            in_specs.append(pl.BlockSpec((1, tn), lambda i, j, k: (0, j)))
            blocks.append(_nbytes((SUBLANES, tn), arr.dtype))
        elif kind == "row":
            in_specs.append(pl.BlockSpec((tm, arr.shape[1]), lambda i, j, k: (i, 0)))
            blocks.append(_nbytes((tm, arr.shape[1]), arr.dtype))
        else:
            in_specs.append(pl.BlockSpec(arr.shape, lambda i, j, k: (0,) * arr.ndim))
            blocks.append(_nbytes(arr.shape, arr.dtype))
        arrays.append(arr)
    out_specs, out_shape = [], []
    for n_tot, n_blk, dt in outs:
        n_blk = tn if n_blk is None else n_blk
        out_specs.append(pl.BlockSpec((tm, n_blk), lambda i, j, k: (i, j)))
        out_shape.append(jax.ShapeDtypeStruct((M, n_tot), dt))
        blocks.append(_nbytes((tm, n_blk), dt))
    scratch = [pltpu.VMEM((tm, tn), F32)] if nk > 1 else []
    return pl.pallas_call(
        functools.partial(_mm_kernel, nk=nk, n_extra=len(arrays), n_out=len(outs), epilogue=epilogue),
        grid=(M // tm, N // tn, nk),
        in_specs=in_specs,
        out_specs=out_specs,
        out_shape=out_shape,
        scratch_shapes=scratch,
        compiler_params=pltpu.CompilerParams(
            dimension_semantics=("arbitrary", "arbitrary", "arbitrary"),
            vmem_limit_bytes=_vmem_limit(blocks, 2 * _nbytes((tm, tn), F32))),
        name=name,
    )(x, w, *arrays)


def _mm_ws_kernel(x_ref, *rest, n_w, n_extra, n_out, epilogue):
    w_refs = rest[:n_w]
    extras = rest[n_w:n_w + n_extra]
    outs = rest[n_w + n_extra:n_w + n_extra + n_out]
    wbf = rest[n_w + n_extra + n_out:]

    @pl.when(pl.program_id(1) == 0)
    def _():
        for w_ref, wb in zip(w_refs, wbf):
            wb[...] = w_ref[0].astype(BF16)

    x = x_ref[...]
    prods = [jnp.dot(x, wb[...], preferred_element_type=F32) for wb in wbf]
    epilogue(prods[0] if n_w == 1 else prods, extras, outs)


def _mm_ws_call(x, ws, layer, col0, n_cols, epilogue, outs, *, extras=(), tm=1024, tn=512, name="matmul_ws"):
    M, K = x.shape
    tm, tn = _tile(M, tm), _tile(n_cols, tn)
    assert col0 % tn == 0
    off = col0 // tn
    in_specs = [pl.BlockSpec((tm, K), lambda j, i: (i, 0))]
    in_specs += [pl.BlockSpec((1, K, tn), lambda j, i: (layer, 0, j + off))] * len(ws)
    blocks = [_nbytes((tm, K), x.dtype)] + [_nbytes((K, tn), F32)] * len(ws)
    arrays = []
    for arr, kind in extras:
        if kind == "col":
            in_specs.append(pl.BlockSpec((1, tn), lambda j, i: (0, j)))
            blocks.append(_nbytes((SUBLANES, tn), arr.dtype))
        elif kind == "row":
            in_specs.append(pl.BlockSpec((tm, arr.shape[1]), lambda j, i: (i, 0)))
            blocks.append(_nbytes((tm, arr.shape[1]), arr.dtype))
        else:
            in_specs.append(pl.BlockSpec(arr.shape, lambda j, i: (0,) * arr.ndim))
            blocks.append(_nbytes(arr.shape, arr.dtype))
        arrays.append(arr)
    out_specs, out_shape = [], []
    for n_tot, n_blk, dt in outs:
        n_blk = tn if n_blk is None else n_blk
        out_specs.append(pl.BlockSpec((tm, n_blk), lambda j, i: (i, j)))
        out_shape.append(jax.ShapeDtypeStruct((M, n_tot), dt))
        blocks.append(_nbytes((tm, n_blk), dt))
    scratch_bytes = len(ws) * _nbytes((K, tn), BF16)
    return pl.pallas_call(
        functools.partial(_mm_ws_kernel, n_w=len(ws), n_extra=len(arrays), n_out=len(outs), epilogue=epilogue),
        grid=(n_cols // tn, M // tm),
        in_specs=in_specs,
        out_specs=out_specs,
        out_shape=out_shape,
        scratch_shapes=[pltpu.VMEM((K, tn), BF16)] * len(ws),
        compiler_params=pltpu.CompilerParams(
            dimension_semantics=("arbitrary", "arbitrary"),
            vmem_limit_bytes=_vmem_limit(blocks, scratch_bytes + (len(ws) + 1) * _nbytes((tm, tn), F32))),
        name=name,
    )(x, *ws, *arrays)


def _ep_swiglu(acc, extras, outs):
    g, u = acc
    outs[0][...] = (_silu(g) * u).astype(BF16)


def _ep_f32(acc, extras, outs):
    outs[0][...] = acc


def _ep_bf16(acc, extras, outs):
    outs[0][...] = acc.astype(BF16)


def _ep_silu(acc, extras, outs):
    outs[0][...] = _silu(acc).astype(BF16)


def _ep_forget(acc, extras, outs):
    lb = extras[0][...]
    f = lb + (1.0 - lb) * _sigmoid(acc)
    outs[0][...] = jnp.log2(f)
    outs[1][...] = (1.0 - f).astype(BF16)


def _ep_rms(acc, extras, outs):
    outs[0][...] = _rms(acc, extras[0][...]).astype(BF16)


def _ep_q_rope(acc, extras, outs):
    cos_t, sin_t = extras[0][...], extras[1][...]
    for h in range(acc.shape[1] // Q_HEAD_PAD):
        lo = h * Q_HEAD_PAD
        outs[0][:, lo:lo + LANES] = (acc[:, lo:lo + LANES] * Q_SCALE_LOG2).astype(BF16)
        pe = _rope_block(acc[:, lo + LANES:lo + Q_HEAD_PAD], cos_t, sin_t)
        outs[0][:, lo + LANES:lo + Q_HEAD_PAD] = (pe * Q_SCALE_LOG2).astype(BF16)


def _ep_dkv(acc, extras, outs):
    g, cos_t, sin_t = extras[0][...], extras[1][...], extras[2][...]
    outs[0][...] = _rms(acc[:, :KV_LORA], g).astype(BF16)
    outs[1][...] = _rope_block(acc[:, KV_LORA:KV_LORA + LANES], cos_t, sin_t).astype(BF16)


def _boundary_cumsum(b, m):
    T, W = b.shape
    if 2 * m >= SUBLANES:
        nb = T // (2 * m)
        b3 = b.reshape(nb, 2 * m, W)
        return jnp.broadcast_to(b3[:, m - 1:m, :], (nb, 2 * m, W)).reshape(T, W)
    b3 = b.reshape(T // SUBLANES, SUBLANES, W)
    sub = lax.broadcasted_iota(jnp.int32, b3.shape, 1)
    r = jnp.broadcast_to(b3[:, m - 1:m, :], b3.shape)
    for blk in range(1, SUBLANES // (2 * m)):
        row = blk * 2 * m + m - 1
        r = jnp.where(sub >= blk * 2 * m, jnp.broadcast_to(b3[:, row:row + 1, :], b3.shape), r)
    return r.reshape(T, W)


def _hgrn_kernel(q_ref, k_ref, v_ref, lf_ref, gs_ref, ng_ref, tril_ref, sgn_ref, rowsel_ref, pmask_ref,
                 o_ref, st_ref, *, n_chunks, nh):
    T = HG_CHUNK

    @pl.when(pl.program_id(1) == 0)
    def _():
        st_ref[...] = jnp.zeros_like(st_ref)

    nt = (((1,), (1,)), ((), ()))
    tn = (((0,), (0,)), ((), ()))
    heads = range(nh)
    cols = [slice(hh * HG_HEAD_DIM, (hh + 1) * HG_HEAD_DIM) for hh in heads]

    def chunk(c, carry):
        rows = pl.ds(pl.multiple_of(c * T, T), T)
        lf = [lf_ref[rows, cs] for cs in cols]
        b = []
        for hh in heads:
            hi = lf[hh].astype(BF16)
            r1 = lf[hh] - hi.astype(F32)
            mid = r1.astype(BF16)
            lo = (r1 - mid.astype(F32)).astype(BF16)
            parts = jnp.dot(tril_ref[...], jnp.concatenate([hi, mid, lo], axis=1),
                            preferred_element_type=F32)
            b.append(parts[:, :LANES] + parts[:, LANES:2 * LANES] + parts[:, 2 * LANES:])
        qb = [q_ref[rows, cs] for cs in cols]
        kb = [k_ref[rows, cs] for cs in cols]
        vb = [v_ref[rows, cs] for cs in cols]
        q = [t.astype(F32) for t in qb]
        k = [t.astype(F32) for t in kb]
        st = [st_ref[hh] for hh in heads]
        o = [lax.dot_general((q[hh] * jnp.exp2(b[hh])).astype(BF16), st[hh].astype(BF16), nt,
                             preferred_element_type=F32) for hh in heads]
        a = [jnp.where(pmask_ref[HG_LEVELS] > 0.5,
                       lax.dot_general(qb[hh], kb[hh], nt, preferred_element_type=F32), 0.0)
             for hh in heads]
        for lvl in range(HG_LEVELS):
            second = rowsel_ref[lvl] > 0.5
            for hh in heads:
                if lvl == 0:
                    arg = lf[hh] * rowsel_ref[0]
                else:
                    arg = (b[hh] - _boundary_cumsum(b[hh], 1 << lvl)) * sgn_ref[lvl]
                x = (jnp.where(second, q[hh], k[hh]) * jnp.exp2(arg)).astype(BF16)
                p = lax.dot_general(x, x, nt, preferred_element_type=F32)
                a[hh] = jnp.where(pmask_ref[lvl] > 0.5, p, a[hh])
        for hh in heads:
            o[hh] = o[hh] + jnp.dot(a[hh].astype(BF16), vb[hh], preferred_element_type=F32)
        for hh in heads:
            b_last = b[hh][T - 1:T, :]
            kt = (k[hh] * jnp.exp2(b_last - b[hh])).astype(BF16)
            st_ref[hh] = st[hh] * jnp.exp2(b_last) + lax.dot_general(vb[hh], kt, tn,
                                                                     preferred_element_type=F32)
        for hh in heads:
            y = _rms(o[hh], ng_ref[...]) * gs_ref[rows, cols[hh]].astype(F32)
            o_ref[rows, cols[hh]] = y.astype(o_ref.dtype)
        return carry

    lax.fori_loop(0, n_chunks, chunk, 0)


def _hgrn_masks():
    T = HG_CHUNK
    t = jnp.arange(T)
    tril = (t[:, None] >= t[None, :]).astype(BF16)
    rowsel, pmask = [], []
    for lvl in range(HG_LEVELS):
        m = 1 << lvl
        second = (t // m) % 2 == 1
        rowsel.append(jnp.broadcast_to(second[:, None], (T, HG_HEAD_DIM)))
        same = (t[:, None] // (2 * m)) == (t[None, :] // (2 * m))
        pmask.append(same & second[:, None] & (~second)[None, :])
    pmask.append(t[:, None] == t[None, :])
    rowsel = jnp.stack(rowsel).astype(F32)
    return tril, 2.0 * rowsel - 1.0, rowsel, jnp.stack(pmask).astype(F32)


def _hgrn_call(q, k, v, lf2, gs, norm_g, *, t_blk=1024, nh=8):
    S, D = q.shape
    H = D // HG_HEAD_DIM
    t_blk = _tile(S, t_blk)
    nh = math.gcd(nh, H)
    assert t_blk % HG_CHUNK == 0
    tril, sgn, rowsel, pmask = _hgrn_masks()
    w_blk = nh * HG_HEAD_DIM
    seq = pl.BlockSpec((t_blk, w_blk), lambda h, s: (s, h))

    def const(a):
        return pl.BlockSpec(a.shape, lambda h, s: (0,) * a.ndim)

    ng = norm_g.reshape(1, HG_HEAD_DIM)
    return pl.pallas_call(
        functools.partial(_hgrn_kernel, n_chunks=t_blk // HG_CHUNK, nh=nh),
        grid=(H // nh, S // t_blk),
        in_specs=[seq, seq, seq, seq, seq, const(ng), const(tril), const(sgn), const(rowsel), const(pmask)],
        out_specs=seq,
        out_shape=jax.ShapeDtypeStruct((S, D), BF16),
        scratch_shapes=[pltpu.VMEM((nh, HG_HEAD_DIM, HG_HEAD_DIM), F32)],
        compiler_params=pltpu.CompilerParams(
            dimension_semantics=("arbitrary", "arbitrary"),
            vmem_limit_bytes=_vmem_limit([5 * _nbytes((t_blk, w_blk), BF16), _nbytes((t_blk, w_blk), F32)],
                                         64 * nh * _nbytes((HG_CHUNK, HG_HEAD_DIM), F32))),
        name="hgrn2_recurrence",
    )(q, k, v, lf2, gs, ng, tril, sgn, rowsel, pmask)


def _side_cast(step, w_ref, o_ref, nb_in, nb_out):
    val = w_ref[0].astype(BF16)
    extra = o_ref.shape[1] - val.shape[1]
    if extra:
        val = jnp.concatenate([val, jnp.zeros((val.shape[0], extra), BF16)], axis=1)
    if nb_out != nb_in:
        val = jnp.where(jnp.minimum(step, nb_out - 1) < nb_in, val, jnp.zeros_like(val))
    o_ref[...] = val


def _side_plan(w, layer, steps, pad_to, by_rows, index_of_step):
    L, K, N = w.shape
    if by_rows:
        assert K % steps == 0 and (K // steps) % 16 == 0
        rb, nb_in, nb_out = K // steps, steps, steps
        view, width_in, width_out, rows_out = w, N, pad_to, K
    else:
        assert K % 4 == 0 and pad_to % 4 == 0
        rows_in, rows_out, width_in = K // 4, pad_to // 4, 4 * N
        width_out = width_in
        rb = next(r for r in range(16, rows_out + 1, 16)
                  if rows_in % r == 0 and rows_out % r == 0 and rows_out // r <= steps)
        nb_in, nb_out = rows_in // rb, rows_out // rb
        view = w.reshape(L, rows_in, width_in)
    in_spec = pl.BlockSpec((1, rb, width_in),
                           lambda *g: (layer, jnp.minimum(index_of_step(*g), nb_in - 1), 0))
    out_spec = pl.BlockSpec((rb, width_out), lambda *g: (jnp.minimum(index_of_step(*g), nb_out - 1), 0))
    out_shape = jax.ShapeDtypeStruct((rows_out, width_out), BF16)
    nbytes = _nbytes((rb, width_in), F32) + _nbytes((rb, width_out), BF16)
    return view, in_spec, out_spec, out_shape, (nb_in, nb_out), nbytes


def _attn_kernel(q_ref, kn_ref, v_ref, kpe_ref, *rest, tq, tk, tk_diag, nsub, side_blocks):
    n_side = len(side_blocks)
    side_in = rest[:n_side]
    o_ref = rest[n_side]
    side_out = rest[n_side + 1:2 * n_side + 1]
    m_ref, acc_ref = rest[2 * n_side + 1:]
    qi = pl.program_id(1)
    step = pl.program_id(0) * pl.num_programs(1) + qi
    tb = tq * nsub
    kv_per_block = tb // tk
    nt = (((1,), (1,)), ((), ()))
    m_ref[...] = jnp.full_like(m_ref, -jnp.inf)
    acc_ref[...] = jnp.zeros_like(acc_ref)

    def tile_update(a, kc, ve, mask_off):
        q = q_ref[a * tq:(a + 1) * tq, :]
        s = lax.dot_general(q, kc, nt, preferred_element_type=F32)
        if mask_off is not None:
            r = lax.broadcasted_iota(jnp.int32, s.shape, 0)
            c = lax.broadcasted_iota(jnp.int32, s.shape, 1)
            s = jnp.where(r + mask_off >= c, s, -jnp.inf)
        m_old = m_ref[a]
        m_new = jnp.maximum(m_old, s.max(axis=-1, keepdims=True))
        alpha = jnp.exp2(m_old - m_new)
        p = jnp.concatenate([jnp.exp2(s[:, cb * LANES:(cb + 1) * LANES] - m_new)
                             for cb in range(s.shape[1] // LANES)], axis=1).astype(BF16)
        pv = jnp.dot(p, ve, preferred_element_type=F32)
        acc_ref[a] = jnp.concatenate([alpha, alpha], axis=1) * acc_ref[a] + pv
        m_ref[a] = m_new

    def load_kv(start, size):
        rows = pl.ds(pl.multiple_of(start, size), size)
        kc = jnp.concatenate([kn_ref[rows, :], kpe_ref[rows, :]], axis=1)
        ve = jnp.concatenate([v_ref[rows, :], jnp.ones((size, LANES), BF16)], axis=1)
        return kc, ve

    def body(j, carry):
        kc, ve = load_kv(j * tk, tk)
        for a in range(nsub):
            tile_update(a, kc, ve, None)
        return carry

    lax.fori_loop(0, qi * kv_per_block, body, 0)
    for b in range(tb // tk_diag):
        kc, ve = load_kv(qi * tb + b * tk_diag, tk_diag)
        for a in range(nsub):
            if b * tk_diag > (a + 1) * tq - 1:
                continue
            unmasked = (b + 1) * tk_diag - 1 <= a * tq
            tile_update(a, kc, ve, None if unmasked else a * tq - b * tk_diag)
    for w_ref, c_ref, (nb_in, nb_out) in zip(side_in, side_out, side_blocks):
        _side_cast(step, w_ref, c_ref, nb_in, nb_out)
    for a in range(nsub):
        acc = acc_ref[a]
        o_ref[a * tq:(a + 1) * tq, :] = (acc[:, :V_HEAD_DIM] / acc[:, V_HEAD_DIM:]).astype(o_ref.dtype)


def _attn_call(q, kv, kpe, side=(), *, tq=256, tk=2048, tk_diag=256, nsub=8):
    S = q.shape[0]
    H = q.shape[1] // Q_HEAD_PAD
    nsub = min(nsub, S // tq)
    tb = tq * nsub
    tk = _tile(tb, tk)
    assert S % tb == 0 and tb % tk == 0 and tb % tk_diag == 0
    n_i = S // tb
    plans = [_side_plan(w, layer, H * n_i, pad_to, by_rows, lambda h, i: h * n_i + i)
             for w, layer, pad_to, by_rows in side]
    out_row = pl.BlockSpec((tb, V_HEAD_DIM), lambda h, i: (i, h))
    res = pl.pallas_call(
        functools.partial(_attn_kernel, tq=tq, tk=tk, tk_diag=tk_diag, nsub=nsub,
                          side_blocks=tuple(p[4] for p in plans)),
        grid=(H, n_i),
        in_specs=[
            pl.BlockSpec((tb, Q_HEAD_PAD), lambda h, i: (i, h)),
            pl.BlockSpec((S, NOPE_DIM), lambda h, i: (0, 2 * h)),
            pl.BlockSpec((S, V_HEAD_DIM), lambda h, i: (0, 2 * h + 1)),
            pl.BlockSpec((S, LANES), lambda h, i: (0, 0)),
        ] + [p[1] for p in plans],
        out_specs=[out_row] + [p[2] for p in plans],
        out_shape=[jax.ShapeDtypeStruct((S, H * V_HEAD_DIM), BF16)] + [p[3] for p in plans],
        scratch_shapes=[pltpu.VMEM((nsub, tq, LANES), F32), pltpu.VMEM((nsub, tq, 2 * LANES), F32)],
        compiler_params=pltpu.CompilerParams(
            dimension_semantics=("arbitrary", "arbitrary"),
            vmem_limit_bytes=_vmem_limit([3 * _nbytes((S, LANES), BF16), _nbytes((tb, Q_HEAD_PAD), BF16),
                                          _nbytes((tb, V_HEAD_DIM), BF16)] + [p[5] for p in plans],
                                         2 * nsub * _nbytes((tq, tk), F32))),
        name="mla_attention",
    )(q, kv, kv, kpe, *[p[0] for p in plans])
    return res[0], list(res[1:])


def _ffn_up_kernel(x_ref, wg_ref, wu_ref, o_ref):
    x = x_ref[...]
    g = jnp.dot(x, wg_ref[...], preferred_element_type=F32)
    u = jnp.dot(x, wu_ref[...], preferred_element_type=F32)
    o_ref[...] = (_silu(g) * u).astype(o_ref.dtype)


def _ffn_up_call(h, wg, wu, *, tm=1024, tn=512):
    M, K = h.shape
    N = wg.shape[1]
    tm, tn = _tile(M, tm), _tile(N, tn)
    wspec = pl.BlockSpec((K, tn), lambda i, j: (0, j))
    return pl.pallas_call(
        _ffn_up_kernel,
        grid=(M // tm, N // tn),
        in_specs=[pl.BlockSpec((tm, K), lambda i, j: (i, 0)), wspec, wspec],
        out_specs=pl.BlockSpec((tm, tn), lambda i, j: (i, j)),
        out_shape=jax.ShapeDtypeStruct((M, N), BF16),
        compiler_params=pltpu.CompilerParams(
            dimension_semantics=("arbitrary", "arbitrary"),
            vmem_limit_bytes=_vmem_limit([_nbytes((tm, K), BF16), 2 * _nbytes((K, tn), BF16),
                                          _nbytes((tm, tn), BF16)], 4 * _nbytes((tm, tn), F32))),
        name="ffn_up",
    )(h, wg, wu)


def _pad_cols(w, n):
    return jnp.pad(w, ((0, 0), (0, n - w.shape[1])))


FFN_PAD = 512


def _swiglu(h, layer, w_gate, w_up, w_down, hosted=None):
    F, D = w_gate.shape[2], w_gate.shape[1]
    if hosted is None:
        (a,) = _mm_ws_call(h, [w_gate, w_up], layer, 0, F, _ep_swiglu, [(F, None, BF16)], tn=256, name="ffn_up")
        (y,) = _mm_call(a, w_down[layer].astype(BF16), _ep_f32, [(D, None, F32)],
                        tk=F // 2 if F % (2 * LANES) == 0 else F, name="ffn_down")
        return y
    wg, wu, wd = hosted
    Fp = wd.shape[0]
    a = _ffn_up_call(h, wg, wu, tn=FFN_PAD)
    (y,) = _mm_call(a, wd, _ep_f32, [(D, None, F32)], tn=1024,
                    tk=Fp // 4 if Fp % (4 * 2 * LANES) == 0 else Fp, name="ffn_down")
    return y


def kernel(x, c, positions, ada_w, ada_b, norm_g, ffn_w_gate, ffn_w_up, ffn_w_down, hg_w_in, hg_lb_logits, hg_norm_g, hg_w_out, mla_w_dq, mla_q_norm_g, mla_w_uq, mla_w_o, kv_norm_in_g, kv_w_dkv, kv_norm_g, kv_w_ukv):
    B, S, D = x.shape
    assert B == 1
    depth = ada_w.shape[0]
    n_a = hg_w_in.shape[0]
    mla_heads = kv_w_ukv.shape[1] // (NOPE_DIM + V_HEAD_DIM)
    fdim = hg_lb_logits.shape[1]

    xs = x.reshape(S, D)
    mod = _ada_call(c, ada_w, ada_b).reshape(depth, 6, D)
    cos_t, sin_t = _rope_call(positions)
    lower_bounds = jnp.cumsum(jax.nn.softmax(hg_lb_logits.astype(F32), axis=0), axis=0)

    ffn_dim = ffn_w_gate.shape[2]
    ffn_pad = -(-ffn_dim // FFN_PAD) * FFN_PAD

    h = _prenorm_call(xs, norm_g[0, 0], mod[0, 1], mod[0, 0])
    kv = kpe = None
    for l in range(depth):
        sh_m, sc_m, g_m, sh_f, sc_f, g_f = (mod[l, i] for i in range(6))
        hosted = None
        if l < n_a:
            lb = lower_bounds[l].reshape(1, fdim)
            w_in = [hg_w_in]
            (q,) = _mm_ws_call(h, w_in, l, 0, fdim, _ep_silu, [(fdim, None, BF16)], name="hg_q")
            lf, k = _mm_ws_call(h, w_in, l, fdim, fdim, _ep_forget, [(fdim, None, F32), (fdim, None, BF16)],
                                extras=[(lb, "col")], name="hg_f")
            (v,) = _mm_ws_call(h, w_in, l, 2 * fdim, D, _ep_bf16, [(D, None, BF16)], name="hg_i")
            (gs,) = _mm_ws_call(h, w_in, l, 2 * fdim + D, D, _ep_silu, [(D, None, BF16)], name="hg_g")
            o = _hgrn_call(q, k, v, lf, gs, hg_norm_g[l])
            (y,) = _mm_ws_call(o, [hg_w_out], l, 0, D, _ep_f32, [(D, None, F32)], name="hg_out")
        else:
            j = l - n_a
            q_lora = mla_w_dq.shape[2]
            (qd,) = _mm_call(h, mla_w_dq[j].astype(BF16), _ep_rms, [(q_lora, q_lora, BF16)],
                             extras=[(mla_q_norm_g[j].reshape(1, q_lora), "col")], tn=q_lora, name="mla_dq")
            w_uq = mla_w_uq[j].reshape(q_lora, mla_heads, QK_HEAD_DIM)
            w_uq = jnp.pad(w_uq, ((0, 0), (0, 0), (0, Q_HEAD_PAD - QK_HEAD_DIM)))
            w_uq = w_uq.reshape(q_lora, mla_heads * Q_HEAD_PAD).astype(BF16)
            (q,) = _mm_call(qd, w_uq, _ep_q_rope, [(w_uq.shape[1], None, BF16)],
                            extras=[(cos_t, "row"), (sin_t, "row")], tn=2048, name="mla_uq")
            k_o = mla_w_o.shape[1]
            o, (wg, wu, wd, wo) = _attn_call(q, kv, kpe, side=[
                (ffn_w_gate, l, ffn_pad, True), (ffn_w_up, l, ffn_pad, True),
                (ffn_w_down, l, ffn_pad, False), (mla_w_o, j, k_o, False)])
            hosted = (wg, wu, wd.reshape(ffn_pad, D))
            (y,) = _mm_call(o, wo.reshape(k_o, D), _ep_f32, [(D, None, F32)], name="mla_o")
        xs, h = _resid_call(xs, y, g_m, norm_g[l, 1], [(norm_g[l, 2], sc_f, sh_f)])
        y = _swiglu(h, l, ffn_w_gate, ffn_w_up, ffn_w_down, hosted)
        norms = []
        if l + 1 < depth:
            norms.append((norm_g[l + 1, 0], mod[l + 1, 1], mod[l + 1, 0]))
        if l == n_a - 1:
            norms.append((kv_norm_in_g,))
        res = _resid_call(xs, y, g_f, norm_g[l, 3], norms)
        xs = res[0]
        if l + 1 < depth:
            h = res[1]
        if l == n_a - 1:
            src = res[-1]
            n_dkv = KV_LORA + LANES
            ckvn, kpe = _mm_call(src, _pad_cols(kv_w_dkv, n_dkv).astype(BF16), _ep_dkv,
                                 [(KV_LORA, KV_LORA, BF16), (LANES, LANES, BF16)],
                                 extras=[(kv_norm_g.reshape(1, KV_LORA), "full"), (cos_t, "row"), (sin_t, "row")],
                                 tn=n_dkv, name="mla_dkv")
            (kv,) = _mm_call(ckvn, kv_w_ukv.astype(BF16), _ep_bf16, [(kv_w_ukv.shape[1], None, BF16)],
                             tm=2048, tn=1024, name="mla_ukv")
    return xs.reshape(B, S, D)
```

```python
import functools
import math

import jax
import jax.numpy as jnp
from jax import lax
from jax.experimental import pallas as pl
from jax.experimental.pallas import tpu as pltpu

F32 = jnp.float32
BF16 = jnp.bfloat16

HG_HEAD_DIM = 128
NOPE_DIM = 128
ROPE_DIM = 64
V_HEAD_DIM = 128
QK_HEAD_DIM = NOPE_DIM + ROPE_DIM
KV_LORA = 512
ROPE_THETA = 10000.0
NORM_EPS = 1e-6
ATTN_SCALE = 1.0 / math.sqrt(QK_HEAD_DIM)
Q_SCALE_LOG2 = ATTN_SCALE * math.log2(math.e)

LANES = 128
SUBLANES = 8
VMEM_BYTES_V7X = 64 * 1024 * 1024
VMEM_LIMIT_CAP = VMEM_BYTES_V7X - 8 * 1024 * 1024
VMEM_LIMIT_FLOOR = 32 * 1024 * 1024

Q_HEAD_PAD = 2 * LANES
HG_CHUNK = 128
HG_LEVELS = 7


def _vmem_limit(block_bytes, temp_bytes=0):
    est = 2 * sum(block_bytes) + temp_bytes + (4 << 20)
    return int(min(max(est, VMEM_LIMIT_FLOOR), VMEM_LIMIT_CAP))


def _nbytes(shape, dtype):
    return math.prod(shape) * jnp.dtype(dtype).itemsize


def _tile(dim, pref):
    if dim <= pref:
        return dim
    t = (pref // LANES) * LANES
    while t >= LANES:
        if dim % t == 0:
            return t
        t -= LANES
    return dim


def _sigmoid(x):
    return 1.0 / (1.0 + jnp.exp(-x))


def _silu(x):
    return x * _sigmoid(x)


def _rms(x, g):
    ms = jnp.mean(x * x, axis=-1, keepdims=True)
    return x * lax.rsqrt(ms + NORM_EPS) * g


def _ada_kernel(c_ref, w_ref, b_ref, o_ref, acc_ref, *, nk):
    k = pl.program_id(2)
    tk, tn = w_ref.shape[1], w_ref.shape[2]
    p = w_ref[0] * _silu(c_ref[...])
    part = p.reshape(tk // SUBLANES, SUBLANES, tn).sum(axis=0)

    @pl.when(k == 0)
    def _():
        acc_ref[...] = part

    @pl.when(k > 0)
    def _():
        acc_ref[...] += part

    @pl.when(k == nk - 1)
    def _():
        o_ref[0] = acc_ref[...].sum(axis=0, keepdims=True) + b_ref[0]


def _ada_call(c, ada_w, ada_b):
    L, D, N = ada_w.shape
    tk, tn = _tile(D, 1024), _tile(N, 2048)
    nk = D // tk
    c_col = c.reshape(D, 1)
    b3 = ada_b.reshape(L, 1, N)
    return pl.pallas_call(
        functools.partial(_ada_kernel, nk=nk),
        grid=(L, N // tn, nk),
        in_specs=[
            pl.BlockSpec((tk, 1), lambda l, j, k: (k, 0)),
            pl.BlockSpec((1, tk, tn), lambda l, j, k: (l, k, j)),
            pl.BlockSpec((1, 1, tn), lambda l, j, k: (l, 0, j)),
        ],
        out_specs=pl.BlockSpec((1, 1, tn), lambda l, j, k: (l, 0, j)),
        out_shape=jax.ShapeDtypeStruct((L, 1, N), F32),
        scratch_shapes=[pltpu.VMEM((SUBLANES, tn), F32)],
        compiler_params=pltpu.CompilerParams(
            dimension_semantics=("arbitrary", "arbitrary", "arbitrary"),
            vmem_limit_bytes=_vmem_limit([_nbytes((tk, tn), F32), _nbytes((tk, LANES), F32)],
                                         _nbytes((tk, tn), F32))),
        name="ada_gemv",
    )(c_col, ada_w, b3)


def _rope_kernel(pos_ref, inv_ref, cmask_ref, sgn_ref, cos_ref, sin_ref):
    ang = pos_ref[...].astype(F32) * inv_ref[...]
    cos_ref[...] = jnp.cos(ang) * cmask_ref[...]
    sin_ref[...] = jnp.sin(ang) * sgn_ref[...]


def _rope_call(positions):
    S = positions.shape[-1]
    half = ROPE_DIM // 2
    inv_freq = 1.0 / (ROPE_THETA ** (jnp.arange(0, ROPE_DIM, 2, dtype=F32) / ROPE_DIM))
    zeros = jnp.zeros((LANES - ROPE_DIM,), F32)
    inv = jnp.concatenate([inv_freq, inv_freq, zeros]).reshape(1, LANES)
    cmask = jnp.concatenate([jnp.ones((ROPE_DIM,), F32), zeros]).reshape(1, LANES)
    sgn = jnp.concatenate([-jnp.ones((half,), F32), jnp.ones((half,), F32), zeros]).reshape(1, LANES)
    tm = _tile(S, 1024)
    row = pl.BlockSpec((1, LANES), lambda i: (0, 0))
    out = pl.BlockSpec((tm, LANES), lambda i: (i, 0))
    return pl.pallas_call(
        _rope_kernel,
        grid=(S // tm,),
        in_specs=[pl.BlockSpec((tm, 1), lambda i: (i, 0)), row, row, row],
        out_specs=[out, out],
        out_shape=[jax.ShapeDtypeStruct((S, LANES), F32)] * 2,
        compiler_params=pltpu.CompilerParams(dimension_semantics=("arbitrary",)),
        name="rope_tables",
    )(positions.reshape(S, 1), inv, cmask, sgn)


def _rope_block(blk, cos_t, sin_t):
    half = ROPE_DIM // 2
    lane = lax.broadcasted_iota(jnp.int32, blk.shape, 1)
    swapped = jnp.where(lane < half, pltpu.roll(blk, LANES - half, 1), pltpu.roll(blk, half, 1))
    return blk * cos_t + swapped * sin_t


def _prenorm_kernel(x_ref, g_ref, sc_ref, sh_ref, o_ref):
    y = _rms(x_ref[...], g_ref[...])
    o_ref[...] = (y * (1.0 + sc_ref[...]) + sh_ref[...]).astype(o_ref.dtype)


def _prenorm_call(x, g, sc, sh):
    S, D = x.shape
    tm = _tile(S, 256)
    vec = pl.BlockSpec((1, D), lambda i: (0, 0))
    row = pl.BlockSpec((tm, D), lambda i: (i, 0))
    return pl.pallas_call(
        _prenorm_kernel,
        grid=(S // tm,),
        in_specs=[row, vec, vec, vec],
        out_specs=row,
        out_shape=jax.ShapeDtypeStruct((S, D), BF16),
        compiler_params=pltpu.CompilerParams(
            dimension_semantics=("arbitrary",),
            vmem_limit_bytes=_vmem_limit([_nbytes((tm, D), F32), _nbytes((tm, D), BF16)],
                                         3 * _nbytes((tm, D), F32))),
        name="prenorm",
    )(x, g.reshape(1, D), sc.reshape(1, D), sh.reshape(1, D))


def _resid_kernel(*refs, modulated):
    x_ref, y_ref, gate_ref, w_ref = refs[:4]
    n_out = len(modulated)
    n_par = sum(3 if m else 1 for m in modulated)
    par = refs[4:4 + n_par]
    xo_ref = refs[4 + n_par]
    outs = refs[5 + n_par:5 + n_par + n_out]
    xn = x_ref[...] + gate_ref[...] * _rms(y_ref[...], w_ref[...])
    xo_ref[...] = xn
    if n_out:
        inv = lax.rsqrt(jnp.mean(xn * xn, axis=-1, keepdims=True) + NORM_EPS)
        xh = xn * inv
        p = 0
        for m, o_ref in zip(modulated, outs):
            h = xh * par[p][...]
            if m:
                h = h * (1.0 + par[p + 1][...]) + par[p + 2][...]
            p += 3 if m else 1
            o_ref[...] = h.astype(o_ref.dtype)


def _resid_call(x, y, gate, w, norms):
    S, D = x.shape
    tm = _tile(S, 256)
    vec = pl.BlockSpec((1, D), lambda i: (0, 0))
    row = pl.BlockSpec((tm, D), lambda i: (i, 0))
    modulated = tuple(len(n) == 3 for n in norms)
    params = [p.reshape(1, D) for n in norms for p in n]
    n_out = len(norms)
    return pl.pallas_call(
        functools.partial(_resid_kernel, modulated=modulated),
        grid=(S // tm,),
        in_specs=[row, row, vec, vec] + [vec] * len(params),
        out_specs=[row] * (1 + n_out),
        out_shape=[jax.ShapeDtypeStruct((S, D), F32)] + [jax.ShapeDtypeStruct((S, D), BF16)] * n_out,
        compiler_params=pltpu.CompilerParams(
            dimension_semantics=("arbitrary",),
            vmem_limit_bytes=_vmem_limit([_nbytes((tm, D), F32)] * 3 + [_nbytes((tm, D), BF16)] * n_out,
                                         4 * _nbytes((tm, D), F32))),
        name="resid_norm",
    )(x, y, gate.reshape(1, D), w.reshape(1, D), *params)


def _mm_kernel(x_ref, w_ref, *rest, nk, n_extra, n_out, epilogue):
    extras = rest[:n_extra]
    outs = rest[n_extra:n_extra + n_out]
    w = w_ref[0] if len(w_ref.shape) == 3 else w_ref[...]
    prod = jnp.dot(x_ref[...], w, preferred_element_type=F32)
    if nk == 1:
        epilogue(prod, extras, outs)
        return
    acc_ref = rest[n_extra + n_out]
    k = pl.program_id(2)

    @pl.when(k == 0)
    def _():
        acc_ref[...] = prod

    @pl.when(k > 0)
    def _():
        acc_ref[...] += prod

    @pl.when(k == nk - 1)
    def _():
        epilogue(acc_ref[...], extras, outs)


def _mm_call(x, w, epilogue, outs, *, extras=(), layer=None, tm=1024, tn=512, tk=4096, name="matmul"):
    M, K = x.shape
    N = w.shape[-1]
    tm, tn, tk = _tile(M, tm), _tile(N, tn), _tile(K, tk)
    nk = K // tk
    if layer is None:
        w_spec = pl.BlockSpec((tk, tn), lambda i, j, k: (k, j))
    else:
        w_spec = pl.BlockSpec((1, tk, tn), lambda i, j, k: (layer, k, j))
    in_specs = [pl.BlockSpec((tm, tk), lambda i, j, k: (i, k)), w_spec]
    blocks = [_nbytes((tm, tk), x.dtype), _nbytes((tk, tn), w.dtype)]
    arrays = []
    for arr, kind in extras:
        if kind == "col":
            in_specs.append(pl.BlockSpec((1, tn), lambda i, j, k: (0, j)))
            blocks.append(_nbytes((SUBLANES, tn), arr.dtype))
        elif kind == "row":
            in_specs.append(pl.BlockSpec((tm, arr.shape[1]), lambda i, j, k: (i, 0)))
            blocks.append(_nbytes((tm, arr.shape[1]), arr.dtype))
        else:
            in_specs.append(pl.BlockSpec(arr.shape, lambda i, j, k: (0,) * arr.ndim))
            blocks.append(_nbytes(arr.shape, arr.dtype))
        arrays.append(arr)
    out_specs, out_shape = [], []
    for n_tot, n_blk, dt in outs:
        n_blk = tn if n_blk is None else n_blk
        out_specs.append(pl.BlockSpec((tm, n_blk), lambda i, j, k: (i, j)))
        out_shape.append(jax.ShapeDtypeStruct((M, n_tot), dt))
        blocks.append(_nbytes((tm, n_blk), dt))
    scratch = [pltpu.VMEM((tm, tn), F32)] if nk > 1 else []
    return pl.pallas_call(
        functools.partial(_mm_kernel, nk=nk, n_extra=len(arrays), n_out=len(outs), epilogue=epilogue),
        grid=(M // tm, N // tn, nk),
        in_specs=in_specs,
        out_specs=out_specs,
        out_shape=out_shape,
        scratch_shapes=scratch,
        compiler_params=pltpu.CompilerParams(
            dimension_semantics=("arbitrary", "arbitrary", "arbitrary"),
            vmem_limit_bytes=_vmem_limit(blocks, 2 * _nbytes((tm, tn), F32))),
        name=name,
    )(x, w, *arrays)


def _mm_ws_kernel(x_ref, *rest, n_w, n_extra, n_out, epilogue):
    w_refs = rest[:n_w]
    extras = rest[n_w:n_w + n_extra]
    outs = rest[n_w + n_extra:n_w + n_extra + n_out]
    wbf = rest[n_w + n_extra + n_out:]

    @pl.when(pl.program_id(1) == 0)
    def _():
        for w_ref, wb in zip(w_refs, wbf):
            wb[...] = w_ref[0].astype(BF16)

    x = x_ref[...]
    prods = [jnp.dot(x, wb[...], preferred_element_type=F32) for wb in wbf]
    epilogue(prods[0] if n_w == 1 else prods, extras, outs)


def _mm_ws_call(x, ws, layer, col0, n_cols, epilogue, outs, *, extras=(), tm=1024, tn=512, name="matmul_ws"):
    M, K = x.shape
    tm, tn = _tile(M, tm), _tile(n_cols, tn)
    assert col0 % tn == 0
    off = col0 // tn
    in_specs = [pl.BlockSpec((tm, K), lambda j, i: (i, 0))]
    in_specs += [pl.BlockSpec((1, K, tn), lambda j, i: (layer, 0, j + off))] * len(ws)
    blocks = [_nbytes((tm, K), x.dtype)] + [_nbytes((K, tn), F32)] * len(ws)
    arrays = []
    for arr, kind in extras:
        if kind == "col":
            in_specs.append(pl.BlockSpec((1, tn), lambda j, i: (0, j)))
            blocks.append(_nbytes((SUBLANES, tn), arr.dtype))
        elif kind == "row":
            in_specs.append(pl.BlockSpec((tm, arr.shape[1]), lambda j, i: (i, 0)))
            blocks.append(_nbytes((tm, arr.shape[1]), arr.dtype))
        else:
            in_specs.append(pl.BlockSpec(arr.shape, lambda j, i: (0,) * arr.ndim))
            blocks.append(_nbytes(arr.shape, arr.dtype))
        arrays.append(arr)
    out_specs, out_shape = [], []
    for n_tot, n_blk, dt in outs:
        n_blk = tn if n_blk is None else n_blk
        out_specs.append(pl.BlockSpec((tm, n_blk), lambda j, i: (i, j)))
        out_shape.append(jax.ShapeDtypeStruct((M, n_tot), dt))
        blocks.append(_nbytes((tm, n_blk), dt))
    scratch_bytes = len(ws) * _nbytes((K, tn), BF16)
    return pl.pallas_call(
        functools.partial(_mm_ws_kernel, n_w=len(ws), n_extra=len(arrays), n_out=len(outs), epilogue=epilogue),
        grid=(n_cols // tn, M // tm),
        in_specs=in_specs,
        out_specs=out_specs,
        out_shape=out_shape,
        scratch_shapes=[pltpu.VMEM((K, tn), BF16)] * len(ws),
        compiler_params=pltpu.CompilerParams(
            dimension_semantics=("arbitrary", "arbitrary"),
            vmem_limit_bytes=_vmem_limit(blocks, scratch_bytes + (len(ws) + 1) * _nbytes((tm, tn), F32))),
        name=name,
    )(x, *ws, *arrays)


def _ep_swiglu(acc, extras, outs):
    g, u = acc
    outs[0][...] = (_silu(g) * u).astype(BF16)


def _ep_f32(acc, extras, outs):
    outs[0][...] = acc


def _ep_bf16(acc, extras, outs):
    outs[0][...] = acc.astype(BF16)


def _ep_silu(acc, extras, outs):
    outs[0][...] = _silu(acc).astype(BF16)


def _ep_forget(acc, extras, outs):
    lb = extras[0][...]
    f = lb + (1.0 - lb) * _sigmoid(acc)
    outs[0][...] = jnp.log2(f)
    outs[1][...] = (1.0 - f).astype(BF16)


def _ep_rms(acc, extras, outs):
    outs[0][...] = _rms(acc, extras[0][...]).astype(BF16)


def _ep_q_rope(acc, extras, outs):
    cos_t, sin_t = extras[0][...], extras[1][...]
    for h in range(acc.shape[1] // Q_HEAD_PAD):
        lo = h * Q_HEAD_PAD
        outs[0][:, lo:lo + LANES] = (acc[:, lo:lo + LANES] * Q_SCALE_LOG2).astype(BF16)
        pe = _rope_block(acc[:, lo + LANES:lo + Q_HEAD_PAD], cos_t, sin_t)
        outs[0][:, lo + LANES:lo + Q_HEAD_PAD] = (pe * Q_SCALE_LOG2).astype(BF16)


def _ep_dkv(acc, extras, outs):
    g, cos_t, sin_t = extras[0][...], extras[1][...], extras[2][...]
    outs[0][...] = _rms(acc[:, :KV_LORA], g).astype(BF16)
    outs[1][...] = _rope_block(acc[:, KV_LORA:KV_LORA + LANES], cos_t, sin_t).astype(BF16)


def _boundary_cumsum(b, m):
    T, W = b.shape
    if 2 * m >= SUBLANES:
        nb = T // (2 * m)
        b3 = b.reshape(nb, 2 * m, W)
        return jnp.broadcast_to(b3[:, m - 1:m, :], (nb, 2 * m, W)).reshape(T, W)
    b3 = b.reshape(T // SUBLANES, SUBLANES, W)
    sub = lax.broadcasted_iota(jnp.int32, b3.shape, 1)
    r = jnp.broadcast_to(b3[:, m - 1:m, :], b3.shape)
    for blk in range(1, SUBLANES // (2 * m)):
        row = blk * 2 * m + m - 1
        r = jnp.where(sub >= blk * 2 * m, jnp.broadcast_to(b3[:, row:row + 1, :], b3.shape), r)
    return r.reshape(T, W)


def _hgrn_kernel(q_ref, k_ref, v_ref, lf_ref, gs_ref, ng_ref, tril_ref, sgn_ref, rowsel_ref, pmask_ref,
                 o_ref, st_ref, *, n_chunks, nh):
    T = HG_CHUNK

    @pl.when(pl.program_id(1) == 0)
    def _():
        st_ref[...] = jnp.zeros_like(st_ref)

    nt = (((1,), (1,)), ((), ()))
    tn = (((0,), (0,)), ((), ()))
    heads = range(nh)
    cols = [slice(hh * HG_HEAD_DIM, (hh + 1) * HG_HEAD_DIM) for hh in heads]

    def chunk(c, carry):
        rows = pl.ds(pl.multiple_of(c * T, T), T)
        lf = [lf_ref[rows, cs] for cs in cols]
        b = []
        for hh in heads:
            hi = lf[hh].astype(BF16)
            r1 = lf[hh] - hi.astype(F32)
            mid = r1.astype(BF16)
            lo = (r1 - mid.astype(F32)).astype(BF16)
            parts = jnp.dot(tril_ref[...], jnp.concatenate([hi, mid, lo], axis=1),
                            preferred_element_type=F32)
            b.append(parts[:, :LANES] + parts[:, LANES:2 * LANES] + parts[:, 2 * LANES:])
        qb = [q_ref[rows, cs] for cs in cols]
        kb = [k_ref[rows, cs] for cs in cols]
        vb = [v_ref[rows, cs] for cs in cols]
        q = [t.astype(F32) for t in qb]
        k = [t.astype(F32) for t in kb]
        st = [st_ref[hh] for hh in heads]
        o = [lax.dot_general((q[hh] * jnp.exp2(b[hh])).astype(BF16), st[hh].astype(BF16), nt,
                             preferred_element_type=F32) for hh in heads]
        a = [jnp.where(pmask_ref[HG_LEVELS] > 0.5,
                       lax.dot_general(qb[hh], kb[hh], nt, preferred_element_type=F32), 0.0)
             for hh in heads]
        for lvl in range(HG_LEVELS):
            second = rowsel_ref[lvl] > 0.5
            for hh in heads:
                if lvl == 0:
                    arg = lf[hh] * rowsel_ref[0]
                else:
                    arg = (b[hh] - _boundary_cumsum(b[hh], 1 << lvl)) * sgn_ref[lvl]
                x = (jnp.where(second, q[hh], k[hh]) * jnp.exp2(arg)).astype(BF16)
                p = lax.dot_general(x, x, nt, preferred_element_type=F32)
                a[hh] = jnp.where(pmask_ref[lvl] > 0.5, p, a[hh])
        for hh in heads:
            o[hh] = o[hh] + jnp.dot(a[hh].astype(BF16), vb[hh], preferred_element_type=F32)
        for hh in heads:
            b_last = b[hh][T - 1:T, :]
            kt = (k[hh] * jnp.exp2(b_last - b[hh])).astype(BF16)
            st_ref[hh] = st[hh] * jnp.exp2(b_last) + lax.dot_general(vb[hh], kt, tn,
                                                                     preferred_element_type=F32)
        for hh in heads:
            y = _rms(o[hh], ng_ref[...]) * gs_ref[rows, cols[hh]].astype(F32)
            o_ref[rows, cols[hh]] = y.astype(o_ref.dtype)
        return carry

    lax.fori_loop(0, n_chunks, chunk, 0)


def _hgrn_masks():
    T = HG_CHUNK
    t = jnp.arange(T)
    tril = (t[:, None] >= t[None, :]).astype(BF16)
    rowsel, pmask = [], []
    for lvl in range(HG_LEVELS):
        m = 1 << lvl
        second = (t // m) % 2 == 1
        rowsel.append(jnp.broadcast_to(second[:, None], (T, HG_HEAD_DIM)))
        same = (t[:, None] // (2 * m)) == (t[None, :] // (2 * m))
        pmask.append(same & second[:, None] & (~second)[None, :])
    pmask.append(t[:, None] == t[None, :])
    rowsel = jnp.stack(rowsel).astype(F32)
    return tril, 2.0 * rowsel - 1.0, rowsel, jnp.stack(pmask).astype(F32)


def _hgrn_call(q, k, v, lf2, gs, norm_g, *, t_blk=1024, nh=8):
    S, D = q.shape
    H = D // HG_HEAD_DIM
    t_blk = _tile(S, t_blk)
    nh = math.gcd(nh, H)
    assert t_blk % HG_CHUNK == 0
    tril, sgn, rowsel, pmask = _hgrn_masks()
    w_blk = nh * HG_HEAD_DIM
    seq = pl.BlockSpec((t_blk, w_blk), lambda h, s: (s, h))

    def const(a):
        return pl.BlockSpec(a.shape, lambda h, s: (0,) * a.ndim)

    ng = norm_g.reshape(1, HG_HEAD_DIM)
    return pl.pallas_call(
        functools.partial(_hgrn_kernel, n_chunks=t_blk // HG_CHUNK, nh=nh),
        grid=(H // nh, S // t_blk),
        in_specs=[seq, seq, seq, seq, seq, const(ng), const(tril), const(sgn), const(rowsel), const(pmask)],
        out_specs=seq,
        out_shape=jax.ShapeDtypeStruct((S, D), BF16),
        scratch_shapes=[pltpu.VMEM((nh, HG_HEAD_DIM, HG_HEAD_DIM), F32)],
        compiler_params=pltpu.CompilerParams(
            dimension_semantics=("arbitrary", "arbitrary"),
            vmem_limit_bytes=_vmem_limit([5 * _nbytes((t_blk, w_blk), BF16), _nbytes((t_blk, w_blk), F32)],
                                         64 * nh * _nbytes((HG_CHUNK, HG_HEAD_DIM), F32))),
        name="hgrn2_recurrence",
    )(q, k, v, lf2, gs, ng, tril, sgn, rowsel, pmask)


def _side_cast(step, w_ref, o_ref, nb_in, nb_out):
    val = w_ref[0].astype(BF16)
    extra = o_ref.shape[1] - val.shape[1]
    if extra:
        val = jnp.concatenate([val, jnp.zeros((val.shape[0], extra), BF16)], axis=1)
    if nb_out != nb_in:
        val = jnp.where(jnp.minimum(step, nb_out - 1) < nb_in, val, jnp.zeros_like(val))
    o_ref[...] = val


def _side_plan(w, layer, steps, rows_out, cols_out, index_of_step):
    L, K, N = w.shape
    rb = next(r for r in range(16, rows_out + 1, 16)
              if K % r == 0 and rows_out % r == 0 and rows_out // r <= steps)
    nb_in, nb_out = K // rb, rows_out // rb
    in_spec = pl.BlockSpec((1, rb, N), lambda *g: (layer, jnp.minimum(index_of_step(*g), nb_in - 1), 0))
    out_spec = pl.BlockSpec((rb, cols_out), lambda *g: (jnp.minimum(index_of_step(*g), nb_out - 1), 0))
    out_shape = jax.ShapeDtypeStruct((rows_out, cols_out), BF16)
    nbytes = _nbytes((rb, N), F32) + _nbytes((rb, cols_out), BF16)
    return w, in_spec, out_spec, out_shape, (nb_in, nb_out), nbytes


def _attn_kernel(q_ref, kn_ref, v_ref, kpe_ref, *rest, tq, tk, tk_diag, nsub, side_blocks):
    n_side = len(side_blocks)
    side_in = rest[:n_side]
    o_ref = rest[n_side]
    side_out = rest[n_side + 1:2 * n_side + 1]
    m_ref, acc_ref = rest[2 * n_side + 1:]
    qi = pl.program_id(1)
    step = pl.program_id(0) * pl.num_programs(1) + qi
    tb = tq * nsub
    kv_per_block = tb // tk
    nt = (((1,), (1,)), ((), ()))
    m_ref[...] = jnp.full_like(m_ref, -jnp.inf)
    acc_ref[...] = jnp.zeros_like(acc_ref)

    def tile_update(a, kc, ve, mask_off):
        q = q_ref[a * tq:(a + 1) * tq, :]
        s = lax.dot_general(q, kc, nt, preferred_element_type=F32)
        if mask_off is not None:
            r = lax.broadcasted_iota(jnp.int32, s.shape, 0)
            c = lax.broadcasted_iota(jnp.int32, s.shape, 1)
            s = jnp.where(r + mask_off >= c, s, -jnp.inf)
        m_old = m_ref[a]
        m_new = jnp.maximum(m_old, s.max(axis=-1, keepdims=True))
        alpha = jnp.exp2(m_old - m_new)
        p = jnp.concatenate([jnp.exp2(s[:, cb * LANES:(cb + 1) * LANES] - m_new)
                             for cb in range(s.shape[1] // LANES)], axis=1).astype(BF16)
        pv = jnp.dot(p, ve, preferred_element_type=F32)
        acc_ref[a] = jnp.concatenate([alpha, alpha], axis=1) * acc_ref[a] + pv
        m_ref[a] = m_new

    def load_kv(start, size):
        rows = pl.ds(pl.multiple_of(start, size), size)
        kc = jnp.concatenate([kn_ref[rows, :], kpe_ref[rows, :]], axis=1)
        ve = jnp.concatenate([v_ref[rows, :], jnp.ones((size, LANES), BF16)], axis=1)
        return kc, ve

    def body(j, carry):
        kc, ve = load_kv(j * tk, tk)
        for a in range(nsub):
            tile_update(a, kc, ve, None)
        return carry

    lax.fori_loop(0, qi * kv_per_block, body, 0)
    for b in range(tb // tk_diag):
        kc, ve = load_kv(qi * tb + b * tk_diag, tk_diag)
        for a in range(nsub):
            if b * tk_diag > (a + 1) * tq - 1:
                continue
            unmasked = (b + 1) * tk_diag - 1 <= a * tq
            tile_update(a, kc, ve, None if unmasked else a * tq - b * tk_diag)
    for w_ref, c_ref, (nb_in, nb_out) in zip(side_in, side_out, side_blocks):
        _side_cast(step, w_ref, c_ref, nb_in, nb_out)
    for a in range(nsub):
        acc = acc_ref[a]
        o_ref[a * tq:(a + 1) * tq, :] = (acc[:, :V_HEAD_DIM] / acc[:, V_HEAD_DIM:]).astype(o_ref.dtype)


def _attn_call(q, kv, kpe, side=(), *, tq=256, tk=2048, tk_diag=256, nsub=8):
    S = q.shape[0]
    H = q.shape[1] // Q_HEAD_PAD
    nsub = min(nsub, S // tq)
    tb = tq * nsub
    tk = _tile(tb, tk)
    assert S % tb == 0 and tb % tk == 0 and tb % tk_diag == 0
    n_i = S // tb
    plans = [_side_plan(w, layer, H * n_i, rows_out, cols_out, lambda h, i: h * n_i + i)
             for w, layer, rows_out, cols_out in side]
    out_row = pl.BlockSpec((tb, V_HEAD_DIM), lambda h, i: (i, h))
    res = pl.pallas_call(
        functools.partial(_attn_kernel, tq=tq, tk=tk, tk_diag=tk_diag, nsub=nsub,
                          side_blocks=tuple(p[4] for p in plans)),
        grid=(H, n_i),
        in_specs=[
            pl.BlockSpec((tb, Q_HEAD_PAD), lambda h, i: (i, h)),
            pl.BlockSpec((S, NOPE_DIM), lambda h, i: (0, 2 * h)),
            pl.BlockSpec((S, V_HEAD_DIM), lambda h, i: (0, 2 * h + 1)),
            pl.BlockSpec((S, LANES), lambda h, i: (0, 0)),
        ] + [p[1] for p in plans],
        out_specs=[out_row] + [p[2] for p in plans],
        out_shape=[jax.ShapeDtypeStruct((S, H * V_HEAD_DIM), BF16)] + [p[3] for p in plans],
        scratch_shapes=[pltpu.VMEM((nsub, tq, LANES), F32), pltpu.VMEM((nsub, tq, 2 * LANES), F32)],
        compiler_params=pltpu.CompilerParams(
            dimension_semantics=("arbitrary", "arbitrary"),
            vmem_limit_bytes=_vmem_limit([3 * _nbytes((S, LANES), BF16), _nbytes((tb, Q_HEAD_PAD), BF16),
                                          _nbytes((tb, V_HEAD_DIM), BF16)] + [p[5] for p in plans],
                                         2 * nsub * _nbytes((tq, tk), F32))),
        name="mla_attention",
    )(q, kv, kv, kpe, *[p[0] for p in plans])
    return res[0], list(res[1:])


def _ffn_up_kernel(x_ref, wg_ref, wu_ref, o_ref):
    x = x_ref[...]
    g = jnp.dot(x, wg_ref[...], preferred_element_type=F32)
    u = jnp.dot(x, wu_ref[...], preferred_element_type=F32)
    o_ref[...] = (_silu(g) * u).astype(o_ref.dtype)


def _ffn_up_call(h, wg, wu, *, tm=1024, tn=512):
    M, K = h.shape
    N = wg.shape[1]
    tm, tn = _tile(M, tm), _tile(N, tn)
    wspec = pl.BlockSpec((K, tn), lambda i, j: (0, j))
    return pl.pallas_call(
        _ffn_up_kernel,
        grid=(M // tm, N // tn),
        in_specs=[pl.BlockSpec((tm, K), lambda i, j: (i, 0)), wspec, wspec],
        out_specs=pl.BlockSpec((tm, tn), lambda i, j: (i, j)),
        out_shape=jax.ShapeDtypeStruct((M, N), BF16),
        compiler_params=pltpu.CompilerParams(
            dimension_semantics=("arbitrary", "arbitrary"),
            vmem_limit_bytes=_vmem_limit([_nbytes((tm, K), BF16), 2 * _nbytes((K, tn), BF16),
                                          _nbytes((tm, tn), BF16)], 4 * _nbytes((tm, tn), F32))),
        name="ffn_up",
    )(h, wg, wu)


def _pad_cols(w, n):
    return jnp.pad(w, ((0, 0), (0, n - w.shape[1])))


FFN_PAD = 512


def _swiglu(h, layer, w_gate, w_up, w_down, hosted=None):
    F, D = w_gate.shape[2], w_gate.shape[1]
    if hosted is None:
        (a,) = _mm_ws_call(h, [w_gate, w_up], layer, 0, F, _ep_swiglu, [(F, None, BF16)], tn=256, name="ffn_up")
        (y,) = _mm_call(a, w_down[layer].astype(BF16), _ep_f32, [(D, None, F32)],
                        tk=F // 2 if F % (2 * LANES) == 0 else F, name="ffn_down")
        return y
    wg, wu, wd = hosted
    Fp = wd.shape[0]
    a = _ffn_up_call(h, wg, wu, tn=FFN_PAD)
    (y,) = _mm_call(a, wd, _ep_f32, [(D, None, F32)], tn=1024,
                    tk=Fp // 4 if Fp % (4 * 2 * LANES) == 0 else Fp, name="ffn_down")
    return y


def kernel(x, c, positions, ada_w, ada_b, norm_g, ffn_w_gate, ffn_w_up, ffn_w_down, hg_w_in, hg_lb_logits, hg_norm_g, hg_w_out, mla_w_dq, mla_q_norm_g, mla_w_uq, mla_w_o, kv_norm_in_g, kv_w_dkv, kv_norm_g, kv_w_ukv):
    B, S, D = x.shape
    assert B == 1
    depth = ada_w.shape[0]
    n_a = hg_w_in.shape[0]
    mla_heads = kv_w_ukv.shape[1] // (NOPE_DIM + V_HEAD_DIM)
    fdim = hg_lb_logits.shape[1]

    xs = x.reshape(S, D)
    mod = _ada_call(c, ada_w, ada_b).reshape(depth, 6, D)
    cos_t, sin_t = _rope_call(positions)
    lower_bounds = jnp.cumsum(jax.nn.softmax(hg_lb_logits.astype(F32), axis=0), axis=0)

    ffn_dim = ffn_w_gate.shape[2]
    ffn_pad = -(-ffn_dim // FFN_PAD) * FFN_PAD

    h = _prenorm_call(xs, norm_g[0, 0], mod[0, 1], mod[0, 0])
    kv = kpe = None
    for l in range(depth):
        sh_m, sc_m, g_m, sh_f, sc_f, g_f = (mod[l, i] for i in range(6))
        hosted = None
        if l < n_a:
            lb = lower_bounds[l].reshape(1, fdim)
            w_in = [hg_w_in]
            (q,) = _mm_ws_call(h, w_in, l, 0, fdim, _ep_silu, [(fdim, None, BF16)], name="hg_q")
            lf, k = _mm_ws_call(h, w_in, l, fdim, fdim, _ep_forget, [(fdim, None, F32), (fdim, None, BF16)],
                                extras=[(lb, "col")], name="hg_f")
            (v,) = _mm_ws_call(h, w_in, l, 2 * fdim, D, _ep_bf16, [(D, None, BF16)], name="hg_i")
            (gs,) = _mm_ws_call(h, w_in, l, 2 * fdim + D, D, _ep_silu, [(D, None, BF16)], name="hg_g")
            o = _hgrn_call(q, k, v, lf, gs, hg_norm_g[l])
            (y,) = _mm_ws_call(o, [hg_w_out], l, 0, D, _ep_f32, [(D, None, F32)], name="hg_out")
        else:
            j = l - n_a
            q_lora = mla_w_dq.shape[2]
            (qd,) = _mm_call(h, mla_w_dq[j].astype(BF16), _ep_rms, [(q_lora, q_lora, BF16)],
                             extras=[(mla_q_norm_g[j].reshape(1, q_lora), "col")], tn=q_lora, name="mla_dq")
            w_uq = mla_w_uq[j].reshape(q_lora, mla_heads, QK_HEAD_DIM)
            w_uq = jnp.pad(w_uq, ((0, 0), (0, 0), (0, Q_HEAD_PAD - QK_HEAD_DIM)))
            w_uq = w_uq.reshape(q_lora, mla_heads * Q_HEAD_PAD).astype(BF16)
            (q,) = _mm_call(qd, w_uq, _ep_q_rope, [(w_uq.shape[1], None, BF16)],
                            extras=[(cos_t, "row"), (sin_t, "row")], tn=2048, name="mla_uq")
            o, (wg, wu, wd, wo) = _attn_call(q, kv, kpe, side=[
                (ffn_w_gate, l, D, ffn_pad), (ffn_w_up, l, D, ffn_pad),
                (ffn_w_down, l, ffn_pad, D), (mla_w_o, j, mla_w_o.shape[1], D)])
            hosted = (wg, wu, wd)
            (y,) = _mm_call(o, wo, _ep_f32, [(D, None, F32)], name="mla_o")
        xs, h = _resid_call(xs, y, g_m, norm_g[l, 1], [(norm_g[l, 2], sc_f, sh_f)])
        y = _swiglu(h, l, ffn_w_gate, ffn_w_up, ffn_w_down, hosted)
        norms = []
        if l + 1 < depth:
            norms.append((norm_g[l + 1, 0], mod[l + 1, 1], mod[l + 1, 0]))
        if l == n_a - 1:
            norms.append((kv_norm_in_g,))
        res = _resid_call(xs, y, g_f, norm_g[l, 3], norms)
        xs = res[0]
        if l + 1 < depth:
            h = res[1]
        if l == n_a - 1:
            src = res[-1]
            n_dkv = KV_LORA + LANES
            ckvn, kpe = _mm_call(src, _pad_cols(kv_w_dkv, n_dkv).astype(BF16), _ep_dkv,
                                 [(KV_LORA, KV_LORA, BF16), (LANES, LANES, BF16)],
                                 extras=[(kv_norm_g.reshape(1, KV_LORA), "full"), (cos_t, "row"), (sin_t, "row")],
                                 tn=n_dkv, name="mla_dkv")
            (kv,) = _mm_call(ckvn, kv_w_ukv.astype(BF16), _ep_bf16, [(kv_w_ukv.shape[1], None, BF16)],
                             tm=2048, tn=1024, name="mla_ukv")
    return xs.reshape(B, S, D)
```

```python
import functools
import math

import jax
import jax.numpy as jnp
from jax import lax
from jax.experimental import pallas as pl
from jax.experimental.pallas import tpu as pltpu

F32 = jnp.float32
BF16 = jnp.bfloat16

HG_HEAD_DIM = 128
NOPE_DIM = 128
ROPE_DIM = 64
V_HEAD_DIM = 128
QK_HEAD_DIM = NOPE_DIM + ROPE_DIM
KV_LORA = 512
ROPE_THETA = 10000.0
NORM_EPS = 1e-6
ATTN_SCALE = 1.0 / math.sqrt(QK_HEAD_DIM)
Q_SCALE_LOG2 = ATTN_SCALE * math.log2(math.e)

LANES = 128
SUBLANES = 8
VMEM_BYTES_V7X = 64 * 1024 * 1024
VMEM_LIMIT_CAP = VMEM_BYTES_V7X - 8 * 1024 * 1024
VMEM_LIMIT_FLOOR = 32 * 1024 * 1024

Q_HEAD_PAD = 2 * LANES
HG_CHUNK = 128
HG_LEVELS = 7


def _vmem_limit(block_bytes, temp_bytes=0):
    est = 2 * sum(block_bytes) + temp_bytes + (4 << 20)
    return int(min(max(est, VMEM_LIMIT_FLOOR), VMEM_LIMIT_CAP))


def _nbytes(shape, dtype):
    return math.prod(shape) * jnp.dtype(dtype).itemsize


def _tile(dim, pref):
    if dim <= pref:
        return dim
    t = (pref // LANES) * LANES
    while t >= LANES:
        if dim % t == 0:
            return t
        t -= LANES
    return dim


def _sigmoid(x):
    return 1.0 / (1.0 + jnp.exp(-x))


def _silu(x):
    return x * _sigmoid(x)


def _rms(x, g):
    ms = jnp.mean(x * x, axis=-1, keepdims=True)
    return x * lax.rsqrt(ms + NORM_EPS) * g


def _ada_kernel(c_ref, w_ref, b_ref, o_ref, acc_ref, *, nk):
    k = pl.program_id(2)
    tk, tn = w_ref.shape[1], w_ref.shape[2]
    p = w_ref[0] * _silu(c_ref[...])
    part = p.reshape(tk // SUBLANES, SUBLANES, tn).sum(axis=0)

    @pl.when(k == 0)
    def _():
        acc_ref[...] = part

    @pl.when(k > 0)
    def _():
        acc_ref[...] += part

    @pl.when(k == nk - 1)
    def _():
        o_ref[0] = acc_ref[...].sum(axis=0, keepdims=True) + b_ref[0]


def _ada_call(c, ada_w, ada_b):
    L, D, N = ada_w.shape
    tk, tn = _tile(D, 1024), _tile(N, 2048)
    nk = D // tk
    c_col = c.reshape(D, 1)
    b3 = ada_b.reshape(L, 1, N)
    return pl.pallas_call(
        functools.partial(_ada_kernel, nk=nk),
        grid=(L, N // tn, nk),
        in_specs=[
            pl.BlockSpec((tk, 1), lambda l, j, k: (k, 0)),
            pl.BlockSpec((1, tk, tn), lambda l, j, k: (l, k, j)),
            pl.BlockSpec((1, 1, tn), lambda l, j, k: (l, 0, j)),
        ],
        out_specs=pl.BlockSpec((1, 1, tn), lambda l, j, k: (l, 0, j)),
        out_shape=jax.ShapeDtypeStruct((L, 1, N), F32),
        scratch_shapes=[pltpu.VMEM((SUBLANES, tn), F32)],
        compiler_params=pltpu.CompilerParams(
            dimension_semantics=("arbitrary", "arbitrary", "arbitrary"),
            vmem_limit_bytes=_vmem_limit([_nbytes((tk, tn), F32), _nbytes((tk, LANES), F32)],
                                         _nbytes((tk, tn), F32))),
        name="ada_gemv",
    )(c_col, ada_w, b3)


def _rope_kernel(pos_ref, inv_ref, cmask_ref, sgn_ref, cos_ref, sin_ref):
    ang = pos_ref[...].astype(F32) * inv_ref[...]
    cos_ref[...] = jnp.cos(ang) * cmask_ref[...]
    sin_ref[...] = jnp.sin(ang) * sgn_ref[...]


def _rope_call(positions):
    S = positions.shape[-1]
    half = ROPE_DIM // 2
    inv_freq = 1.0 / (ROPE_THETA ** (jnp.arange(0, ROPE_DIM, 2, dtype=F32) / ROPE_DIM))
    zeros = jnp.zeros((LANES - ROPE_DIM,), F32)
    inv = jnp.concatenate([inv_freq, inv_freq, zeros]).reshape(1, LANES)
    cmask = jnp.concatenate([jnp.ones((ROPE_DIM,), F32), zeros]).reshape(1, LANES)
    sgn = jnp.concatenate([-jnp.ones((half,), F32), jnp.ones((half,), F32), zeros]).reshape(1, LANES)
    tm = _tile(S, 1024)
    row = pl.BlockSpec((1, LANES), lambda i: (0, 0))
    out = pl.BlockSpec((tm, LANES), lambda i: (i, 0))
    return pl.pallas_call(
        _rope_kernel,
        grid=(S // tm,),
        in_specs=[pl.BlockSpec((tm, 1), lambda i: (i, 0)), row, row, row],
        out_specs=[out, out],
        out_shape=[jax.ShapeDtypeStruct((S, LANES), F32)] * 2,
        compiler_params=pltpu.CompilerParams(dimension_semantics=("arbitrary",)),
        name="rope_tables",
    )(positions.reshape(S, 1), inv, cmask, sgn)


def _rope_block(blk, cos_t, sin_t):
    half = ROPE_DIM // 2
    lane = lax.broadcasted_iota(jnp.int32, blk.shape, 1)
    swapped = jnp.where(lane < half, pltpu.roll(blk, LANES - half, 1), pltpu.roll(blk, half, 1))
    return blk * cos_t + swapped * sin_t


def _prenorm_kernel(x_ref, g_ref, sc_ref, sh_ref, o_ref):
    y = _rms(x_ref[...], g_ref[...])
    o_ref[...] = (y * (1.0 + sc_ref[...]) + sh_ref[...]).astype(o_ref.dtype)


def _prenorm_call(x, g, sc, sh):
    S, D = x.shape
    tm = _tile(S, 256)
    vec = pl.BlockSpec((1, D), lambda i: (0, 0))
    row = pl.BlockSpec((tm, D), lambda i: (i, 0))
    return pl.pallas_call(
        _prenorm_kernel,
        grid=(S // tm,),
        in_specs=[row, vec, vec, vec],
        out_specs=row,
        out_shape=jax.ShapeDtypeStruct((S, D), BF16),
        compiler_params=pltpu.CompilerParams(
            dimension_semantics=("arbitrary",),
            vmem_limit_bytes=_vmem_limit([_nbytes((tm, D), F32), _nbytes((tm, D), BF16)],
                                         3 * _nbytes((tm, D), F32))),
        name="prenorm",
    )(x, g.reshape(1, D), sc.reshape(1, D), sh.reshape(1, D))


def _resid_kernel(*refs, modulated):
    x_ref, y_ref, gate_ref, w_ref = refs[:4]
    n_out = len(modulated)
    n_par = sum(3 if m else 1 for m in modulated)
    par = refs[4:4 + n_par]
    xo_ref = refs[4 + n_par]
    outs = refs[5 + n_par:5 + n_par + n_out]
    xn = x_ref[...] + gate_ref[...] * _rms(y_ref[...].astype(F32), w_ref[...])
    xo_ref[...] = xn
    if n_out:
        inv = lax.rsqrt(jnp.mean(xn * xn, axis=-1, keepdims=True) + NORM_EPS)
        xh = xn * inv
        p = 0
        for m, o_ref in zip(modulated, outs):
            h = xh * par[p][...]
            if m:
                h = h * (1.0 + par[p + 1][...]) + par[p + 2][...]
            p += 3 if m else 1
            o_ref[...] = h.astype(o_ref.dtype)


def _resid_call(x, y, gate, w, norms):
    S, D = x.shape
    tm = _tile(S, 256)
    vec = pl.BlockSpec((1, D), lambda i: (0, 0))
    row = pl.BlockSpec((tm, D), lambda i: (i, 0))
    modulated = tuple(len(n) == 3 for n in norms)
    params = [p.reshape(1, D) for n in norms for p in n]
    n_out = len(norms)
    return pl.pallas_call(
        functools.partial(_resid_kernel, modulated=modulated),
        grid=(S // tm,),
        in_specs=[row, row, vec, vec] + [vec] * len(params),
        out_specs=[row] * (1 + n_out),
        out_shape=[jax.ShapeDtypeStruct((S, D), F32)] + [jax.ShapeDtypeStruct((S, D), BF16)] * n_out,
        compiler_params=pltpu.CompilerParams(
            dimension_semantics=("arbitrary",),
            vmem_limit_bytes=_vmem_limit([_nbytes((tm, D), F32)] * 3 + [_nbytes((tm, D), BF16)] * n_out,
                                         4 * _nbytes((tm, D), F32))),
        name="resid_norm",
    )(x, y, gate.reshape(1, D), w.reshape(1, D), *params)


def _mm_kernel(x_ref, w_ref, *rest, nk, n_extra, n_out, epilogue):
    extras = rest[:n_extra]
    outs = rest[n_extra:n_extra + n_out]
    w = w_ref[0] if len(w_ref.shape) == 3 else w_ref[...]
    prod = jnp.dot(x_ref[...], w, preferred_element_type=F32)
    if nk == 1:
        epilogue(prod, extras, outs)
        return
    acc_ref = rest[n_extra + n_out]
    k = pl.program_id(2)

    @pl.when(k == 0)
    def _():
        acc_ref[...] = prod

    @pl.when(k > 0)
    def _():
        acc_ref[...] += prod

    @pl.when(k == nk - 1)
    def _():
        epilogue(acc_ref[...], extras, outs)


def _mm_call(x, w, epilogue, outs, *, extras=(), layer=None, tm=1024, tn=512, tk=4096, name="matmul"):
    M, K = x.shape
    N = w.shape[-1]
    tm, tn, tk = _tile(M, tm), _tile(N, tn), _tile(K, tk)
    nk = K // tk
    if layer is None:
        w_spec = pl.BlockSpec((tk, tn), lambda i, j, k: (k, j))
    else:
        w_spec = pl.BlockSpec((1, tk, tn), lambda i, j, k: (layer, k, j))
    in_specs = [pl.BlockSpec((tm, tk), lambda i, j, k: (i, k)), w_spec]
    blocks = [_nbytes((tm, tk), x.dtype), _nbytes((tk, tn), w.dtype)]
    arrays = []
    for arr, kind in extras:
        if kind == "col":
            in_specs.append(pl.BlockSpec((1, tn), lambda i, j, k: (0, j)))
            blocks.append(_nbytes((SUBLANES, tn), arr.dtype))
        elif kind == "row":
            in_specs.append(pl.BlockSpec((tm, arr.shape[1]), lambda i, j, k: (i, 0)))
            blocks.append(_nbytes((tm, arr.shape[1]), arr.dtype))
        else:
            in_specs.append(pl.BlockSpec(arr.shape, lambda i, j, k: (0,) * arr.ndim))
            blocks.append(_nbytes(arr.shape, arr.dtype))
        arrays.append(arr)
    out_specs, out_shape = [], []
    for n_tot, n_blk, dt in outs:
        n_blk = tn if n_blk is None else n_blk
        out_specs.append(pl.BlockSpec((tm, n_blk), lambda i, j, k: (i, j)))
        out_shape.append(jax.ShapeDtypeStruct((M, n_tot), dt))
        blocks.append(_nbytes((tm, n_blk), dt))
    scratch = [pltpu.VMEM((tm, tn), F32)] if nk > 1 else []
    return pl.pallas_call(
        functools.partial(_mm_kernel, nk=nk, n_extra=len(arrays), n_out=len(outs), epilogue=epilogue),
        grid=(M // tm, N // tn, nk),
        in_specs=in_specs,
        out_specs=out_specs,
        out_shape=out_shape,
        scratch_shapes=scratch,
        compiler_params=pltpu.CompilerParams(
            dimension_semantics=("arbitrary", "arbitrary", "arbitrary"),
            vmem_limit_bytes=_vmem_limit(blocks, 2 * _nbytes((tm, tn), F32))),
        name=name,
    )(x, w, *arrays)


def _mm_ws_kernel(x_ref, *rest, n_w, n_extra, n_out, epilogue):
    w_refs = rest[:n_w]
    extras = rest[n_w:n_w + n_extra]
    outs = rest[n_w + n_extra:n_w + n_extra + n_out]
    wbf = rest[n_w + n_extra + n_out:]

    @pl.when(pl.program_id(1) == 0)
    def _():
        for w_ref, wb in zip(w_refs, wbf):
            wb[...] = w_ref[0].astype(BF16)

    x = x_ref[...]
    prods = [jnp.dot(x, wb[...], preferred_element_type=F32) for wb in wbf]
    epilogue(prods[0] if n_w == 1 else prods, extras, outs)


def _mm_ws_call(x, ws, layer, col0, n_cols, epilogue, outs, *, extras=(), tm=1024, tn=512, name="matmul_ws"):
    M, K = x.shape
    tm, tn = _tile(M, tm), _tile(n_cols, tn)
    assert col0 % tn == 0
    off = col0 // tn
    in_specs = [pl.BlockSpec((tm, K), lambda j, i: (i, 0))]
    in_specs += [pl.BlockSpec((1, K, tn), lambda j, i: (layer, 0, j + off))] * len(ws)
    blocks = [_nbytes((tm, K), x.dtype)] + [_nbytes((K, tn), F32)] * len(ws)
    arrays = []
    for arr, kind in extras:
        if kind == "col":
            in_specs.append(pl.BlockSpec((1, tn), lambda j, i: (0, j)))
            blocks.append(_nbytes((SUBLANES, tn), arr.dtype))
        elif kind == "row":
            in_specs.append(pl.BlockSpec((tm, arr.shape[1]), lambda j, i: (i, 0)))
            blocks.append(_nbytes((tm, arr.shape[1]), arr.dtype))
        else:
            in_specs.append(pl.BlockSpec(arr.shape, lambda j, i: (0,) * arr.ndim))
            blocks.append(_nbytes(arr.shape, arr.dtype))
        arrays.append(arr)
    out_specs, out_shape = [], []
    for n_tot, n_blk, dt in outs:
        n_blk = tn if n_blk is None else n_blk
        out_specs.append(pl.BlockSpec((tm, n_blk), lambda j, i: (i, j)))
        out_shape.append(jax.ShapeDtypeStruct((M, n_tot), dt))
        blocks.append(_nbytes((tm, n_blk), dt))
    scratch_bytes = len(ws) * _nbytes((K, tn), BF16)
    return pl.pallas_call(
        functools.partial(_mm_ws_kernel, n_w=len(ws), n_extra=len(arrays), n_out=len(outs), epilogue=epilogue),
        grid=(n_cols // tn, M // tm),
        in_specs=in_specs,
        out_specs=out_specs,
        out_shape=out_shape,
        scratch_shapes=[pltpu.VMEM((K, tn), BF16)] * len(ws),
        compiler_params=pltpu.CompilerParams(
            dimension_semantics=("arbitrary", "arbitrary"),
            vmem_limit_bytes=_vmem_limit(blocks, scratch_bytes + (len(ws) + 1) * _nbytes((tm, tn), F32))),
        name=name,
    )(x, *ws, *arrays)


def _ep_swiglu(acc, extras, outs):
    g, u = acc
    outs[0][...] = (_silu(g) * u).astype(BF16)


def _ep_bf16(acc, extras, outs):
    outs[0][...] = acc.astype(BF16)


def _ep_silu(acc, extras, outs):
    outs[0][...] = _silu(acc).astype(BF16)


def _ep_forget(acc, extras, outs):
    lb = extras[0][...]
    f = lb + (1.0 - lb) * _sigmoid(acc)
    outs[0][...] = jnp.log2(f)
    outs[1][...] = (1.0 - f).astype(BF16)


def _ep_rms(acc, extras, outs):
    outs[0][...] = _rms(acc, extras[0][...]).astype(BF16)


def _ep_q_rope(acc, extras, outs):
    cos_t, sin_t = extras[0][...], extras[1][...]
    for h in range(acc.shape[1] // Q_HEAD_PAD):
        lo = h * Q_HEAD_PAD
        outs[0][:, lo:lo + LANES] = (acc[:, lo:lo + LANES] * Q_SCALE_LOG2).astype(BF16)
        pe = _rope_block(acc[:, lo + LANES:lo + Q_HEAD_PAD], cos_t, sin_t)
        outs[0][:, lo + LANES:lo + Q_HEAD_PAD] = (pe * Q_SCALE_LOG2).astype(BF16)


def _uq_pad_kernel(w_ref, o_ref):
    w = w_ref[0]
    z = jnp.zeros((w.shape[0], Q_HEAD_PAD - QK_HEAD_DIM), F32)
    o_ref[...] = jnp.concatenate([w[:, :QK_HEAD_DIM], z, w[:, QK_HEAD_DIM:], z], axis=1).astype(BF16)


def _uq_pad_call(w_uq, layer, heads):
    L, Q, N = w_uq.shape
    assert heads % 2 == 0 and N == heads * QK_HEAD_DIM
    return pl.pallas_call(
        _uq_pad_kernel,
        grid=(heads // 2,),
        in_specs=[pl.BlockSpec((1, Q, 2 * QK_HEAD_DIM), lambda p: (layer, 0, p))],
        out_specs=pl.BlockSpec((Q, 2 * Q_HEAD_PAD), lambda p: (0, p)),
        out_shape=jax.ShapeDtypeStruct((Q, heads * Q_HEAD_PAD), BF16),
        compiler_params=pltpu.CompilerParams(dimension_semantics=("arbitrary",)),
        name="uq_pad",
    )(w_uq)


def _ep_dkv(acc, extras, outs):
    g, cos_t, sin_t = extras[0][...], extras[1][...], extras[2][...]
    outs[0][...] = _rms(acc[:, :KV_LORA], g).astype(BF16)
    outs[1][...] = _rope_block(acc[:, KV_LORA:KV_LORA + LANES], cos_t, sin_t).astype(BF16)


def _boundary_cumsum(b, m):
    T, W = b.shape
    if 2 * m >= SUBLANES:
        nb = T // (2 * m)
        b3 = b.reshape(nb, 2 * m, W)
        return jnp.broadcast_to(b3[:, m - 1:m, :], (nb, 2 * m, W)).reshape(T, W)
    b3 = b.reshape(T // SUBLANES, SUBLANES, W)
    sub = lax.broadcasted_iota(jnp.int32, b3.shape, 1)
    r = jnp.broadcast_to(b3[:, m - 1:m, :], b3.shape)
    for blk in range(1, SUBLANES // (2 * m)):
        row = blk * 2 * m + m - 1
        r = jnp.where(sub >= blk * 2 * m, jnp.broadcast_to(b3[:, row:row + 1, :], b3.shape), r)
    return r.reshape(T, W)


def _hgrn_kernel(q_ref, k_ref, v_ref, lf_ref, gs_ref, ng_ref, tril_ref, sgn_ref, rowsel_ref, pmask_ref,
                 o_ref, st_ref, *, n_chunks, nh):
    T = HG_CHUNK

    @pl.when(pl.program_id(1) == 0)
    def _():
        st_ref[...] = jnp.zeros_like(st_ref)

    nt = (((1,), (1,)), ((), ()))
    tn = (((0,), (0,)), ((), ()))
    heads = range(nh)
    cols = [slice(hh * HG_HEAD_DIM, (hh + 1) * HG_HEAD_DIM) for hh in heads]

    def chunk(c, carry):
        rows = pl.ds(pl.multiple_of(c * T, T), T)
        lf = [lf_ref[rows, cs] for cs in cols]
        b = []
        for hh in heads:
            hi = lf[hh].astype(BF16)
            r1 = lf[hh] - hi.astype(F32)
            mid = r1.astype(BF16)
            lo = (r1 - mid.astype(F32)).astype(BF16)
            parts = jnp.dot(tril_ref[...], jnp.concatenate([hi, mid, lo], axis=1),
                            preferred_element_type=F32)
            b.append(parts[:, :LANES] + parts[:, LANES:2 * LANES] + parts[:, 2 * LANES:])
        qb = [q_ref[rows, cs] for cs in cols]
        kb = [k_ref[rows, cs] for cs in cols]
        vb = [v_ref[rows, cs] for cs in cols]
        q = [t.astype(F32) for t in qb]
        k = [t.astype(F32) for t in kb]
        st = [st_ref[hh] for hh in heads]
        o = [lax.dot_general((q[hh] * jnp.exp2(b[hh])).astype(BF16), st[hh].astype(BF16), nt,
                             preferred_element_type=F32) for hh in heads]
        a = [jnp.where(pmask_ref[HG_LEVELS] > 0.5,
                       lax.dot_general(qb[hh], kb[hh], nt, preferred_element_type=F32), 0.0)
             for hh in heads]
        for lvl in range(HG_LEVELS):
            second = rowsel_ref[lvl] > 0.5
            for hh in heads:
                if lvl == 0:
                    arg = lf[hh] * rowsel_ref[0]
                else:
                    arg = (b[hh] - _boundary_cumsum(b[hh], 1 << lvl)) * sgn_ref[lvl]
                x = (jnp.where(second, q[hh], k[hh]) * jnp.exp2(arg)).astype(BF16)
                p = lax.dot_general(x, x, nt, preferred_element_type=F32)
                a[hh] = jnp.where(pmask_ref[lvl] > 0.5, p, a[hh])
        for hh in heads:
            o[hh] = o[hh] + jnp.dot(a[hh].astype(BF16), vb[hh], preferred_element_type=F32)
        for hh in heads:
            b_last = b[hh][T - 1:T, :]
            kt = (k[hh] * jnp.exp2(b_last - b[hh])).astype(BF16)
            st_ref[hh] = st[hh] * jnp.exp2(b_last) + lax.dot_general(vb[hh], kt, tn,
                                                                     preferred_element_type=F32)
        for hh in heads:
            y = _rms(o[hh], ng_ref[...]) * gs_ref[rows, cols[hh]].astype(F32)
            o_ref[rows, cols[hh]] = y.astype(o_ref.dtype)
        return carry

    lax.fori_loop(0, n_chunks, chunk, 0)


def _hgrn_masks():
    T = HG_CHUNK
    t = jnp.arange(T)
    tril = (t[:, None] >= t[None, :]).astype(BF16)
    rowsel, pmask = [], []
    for lvl in range(HG_LEVELS):
        m = 1 << lvl
        second = (t // m) % 2 == 1
        rowsel.append(jnp.broadcast_to(second[:, None], (T, HG_HEAD_DIM)))
        same = (t[:, None] // (2 * m)) == (t[None, :] // (2 * m))
        pmask.append(same & second[:, None] & (~second)[None, :])
    pmask.append(t[:, None] == t[None, :])
    rowsel = jnp.stack(rowsel).astype(F32)
    return tril, 2.0 * rowsel - 1.0, rowsel, jnp.stack(pmask).astype(F32)


def _hgrn_call(q, k, v, lf2, gs, norm_g, *, t_blk=1024, nh=8):
    S, D = q.shape
    H = D // HG_HEAD_DIM
    t_blk = _tile(S, t_blk)
    nh = math.gcd(nh, H)
    assert t_blk % HG_CHUNK == 0
    tril, sgn, rowsel, pmask = _hgrn_masks()
    w_blk = nh * HG_HEAD_DIM
    seq = pl.BlockSpec((t_blk, w_blk), lambda h, s: (s, h))

    def const(a):
        return pl.BlockSpec(a.shape, lambda h, s: (0,) * a.ndim)

    ng = norm_g.reshape(1, HG_HEAD_DIM)
    return pl.pallas_call(
        functools.partial(_hgrn_kernel, n_chunks=t_blk // HG_CHUNK, nh=nh),
        grid=(H // nh, S // t_blk),
        in_specs=[seq, seq, seq, seq, seq, const(ng), const(tril), const(sgn), const(rowsel), const(pmask)],
        out_specs=seq,
        out_shape=jax.ShapeDtypeStruct((S, D), BF16),
        scratch_shapes=[pltpu.VMEM((nh, HG_HEAD_DIM, HG_HEAD_DIM), F32)],
        compiler_params=pltpu.CompilerParams(
            dimension_semantics=("arbitrary", "arbitrary"),
            vmem_limit_bytes=_vmem_limit([5 * _nbytes((t_blk, w_blk), BF16), _nbytes((t_blk, w_blk), F32)],
                                         64 * nh * _nbytes((HG_CHUNK, HG_HEAD_DIM), F32))),
        name="hgrn2_recurrence",
    )(q, k, v, lf2, gs, ng, tril, sgn, rowsel, pmask)


def _side_cast(step, w_ref, o_ref, nb_in, nb_out):
    val = w_ref[0].astype(BF16)
    extra = o_ref.shape[1] - val.shape[1]
    if extra:
        val = jnp.concatenate([val, jnp.zeros((val.shape[0], extra), BF16)], axis=1)
    if nb_out != nb_in:
        val = jnp.where(jnp.minimum(step, nb_out - 1) < nb_in, val, jnp.zeros_like(val))
    o_ref[...] = val


def _side_plan(w, layer, steps, rows_out, cols_out, index_of_step):
    L, K, N = w.shape
    rb = next(r for r in range(16, rows_out + 1, 16)
              if K % r == 0 and rows_out % r == 0 and rows_out // r <= steps)
    nb_in, nb_out = K // rb, rows_out // rb
    in_spec = pl.BlockSpec((1, rb, N), lambda *g: (layer, jnp.minimum(index_of_step(*g), nb_in - 1), 0))
    out_spec = pl.BlockSpec((rb, cols_out), lambda *g: (jnp.minimum(index_of_step(*g), nb_out - 1), 0))
    out_shape = jax.ShapeDtypeStruct((rows_out, cols_out), BF16)
    nbytes = _nbytes((rb, N), F32) + _nbytes((rb, cols_out), BF16)
    return w, in_spec, out_spec, out_shape, (nb_in, nb_out), nbytes


def _attn_kernel(q_ref, kn_ref, v_ref, kpe_ref, *rest, tq, tk, tk_diag, nsub, side_blocks):
    n_side = len(side_blocks)
    side_in = rest[:n_side]
    o_ref = rest[n_side]
    side_out = rest[n_side + 1:2 * n_side + 1]
    m_ref, acc_ref = rest[2 * n_side + 1:]
    qi = pl.program_id(1)
    step = pl.program_id(0) * pl.num_programs(1) + qi
    tb = tq * nsub
    kv_per_block = tb // tk
    nt = (((1,), (1,)), ((), ()))
    m_ref[...] = jnp.full_like(m_ref, -jnp.inf)
    acc_ref[...] = jnp.zeros_like(acc_ref)

    def tile_update(a, kc, ve, mask_off):
        q = q_ref[a * tq:(a + 1) * tq, :]
        s = lax.dot_general(q, kc, nt, preferred_element_type=F32)
        if mask_off is not None:
            r = lax.broadcasted_iota(jnp.int32, s.shape, 0)
            c = lax.broadcasted_iota(jnp.int32, s.shape, 1)
            s = jnp.where(r + mask_off >= c, s, -jnp.inf)
        m_old = m_ref[a]
        m_new = jnp.maximum(m_old, s.max(axis=-1, keepdims=True))
        alpha = jnp.exp2(m_old - m_new)
        p = jnp.concatenate([jnp.exp2(s[:, cb * LANES:(cb + 1) * LANES] - m_new)
                             for cb in range(s.shape[1] // LANES)], axis=1).astype(BF16)
        pv = jnp.dot(p, ve, preferred_element_type=F32)
        acc_ref[a] = jnp.concatenate([alpha, alpha], axis=1) * acc_ref[a] + pv
        m_ref[a] = m_new

    def load_kv(start, size):
        rows = pl.ds(pl.multiple_of(start, size), size)
        kc = jnp.concatenate([kn_ref[rows, :], kpe_ref[rows, :]], axis=1)
        ve = jnp.concatenate([v_ref[rows, :], jnp.ones((size, LANES), BF16)], axis=1)
        return kc, ve

    def body(j, carry):
        kc, ve = load_kv(j * tk, tk)
        for a in range(nsub):
            tile_update(a, kc, ve, None)
        return carry

    lax.fori_loop(0, qi * kv_per_block, body, 0)
    for b in range(tb // tk_diag):
        kc, ve = load_kv(qi * tb + b * tk_diag, tk_diag)
        for a in range(nsub):
            if b * tk_diag > (a + 1) * tq - 1:
                continue
            unmasked = (b + 1) * tk_diag - 1 <= a * tq
            tile_update(a, kc, ve, None if unmasked else a * tq - b * tk_diag)
    for w_ref, c_ref, (nb_in, nb_out) in zip(side_in, side_out, side_blocks):
        _side_cast(step, w_ref, c_ref, nb_in, nb_out)
    for a in range(nsub):
        acc = acc_ref[a]
        o_ref[a * tq:(a + 1) * tq, :] = (acc[:, :V_HEAD_DIM] / acc[:, V_HEAD_DIM:]).astype(o_ref.dtype)


def _attn_call(q, kv, kpe, side=(), *, tq=256, tk=2048, tk_diag=256, nsub=8):
    S = q.shape[0]
    H = q.shape[1] // Q_HEAD_PAD
    nsub = min(nsub, S // tq)
    tb = tq * nsub
    tk = _tile(tb, tk)
    assert S % tb == 0 and tb % tk == 0 and tb % tk_diag == 0
    n_i = S // tb
    plans = [_side_plan(w, layer, H * n_i, rows_out, cols_out, lambda h, i: h * n_i + i)
             for w, layer, rows_out, cols_out in side]
    out_row = pl.BlockSpec((tb, V_HEAD_DIM), lambda h, i: (i, h))
    res = pl.pallas_call(
        functools.partial(_attn_kernel, tq=tq, tk=tk, tk_diag=tk_diag, nsub=nsub,
                          side_blocks=tuple(p[4] for p in plans)),
        grid=(H, n_i),
        in_specs=[
            pl.BlockSpec((tb, Q_HEAD_PAD), lambda h, i: (i, h)),
            pl.BlockSpec((S, NOPE_DIM), lambda h, i: (0, 2 * h)),
            pl.BlockSpec((S, V_HEAD_DIM), lambda h, i: (0, 2 * h + 1)),
            pl.BlockSpec((S, LANES), lambda h, i: (0, 0)),
        ] + [p[1] for p in plans],
        out_specs=[out_row] + [p[2] for p in plans],
        out_shape=[jax.ShapeDtypeStruct((S, H * V_HEAD_DIM), BF16)] + [p[3] for p in plans],
        scratch_shapes=[pltpu.VMEM((nsub, tq, LANES), F32), pltpu.VMEM((nsub, tq, 2 * LANES), F32)],
        compiler_params=pltpu.CompilerParams(
            dimension_semantics=("arbitrary", "arbitrary"),
            vmem_limit_bytes=_vmem_limit([3 * _nbytes((S, LANES), BF16), _nbytes((tb, Q_HEAD_PAD), BF16),
                                          _nbytes((tb, V_HEAD_DIM), BF16)] + [p[5] for p in plans],
                                         2 * nsub * _nbytes((tq, tk), F32))),
        name="mla_attention",
    )(q, kv, kv, kpe, *[p[0] for p in plans])
    return res[0], list(res[1:])


def _ffn_up_kernel(x_ref, wg_ref, wu_ref, o_ref):
    x = x_ref[...]
    g = jnp.dot(x, wg_ref[...], preferred_element_type=F32)
    u = jnp.dot(x, wu_ref[...], preferred_element_type=F32)
    o_ref[...] = (_silu(g) * u).astype(o_ref.dtype)


def _ffn_up_call(h, wg, wu, *, tm=1024, tn=512):
    M, K = h.shape
    N = wg.shape[1]
    tm, tn = _tile(M, tm), _tile(N, tn)
    wspec = pl.BlockSpec((K, tn), lambda i, j: (0, j))
    return pl.pallas_call(
        _ffn_up_kernel,
        grid=(M // tm, N // tn),
        in_specs=[pl.BlockSpec((tm, K), lambda i, j: (i, 0)), wspec, wspec],
        out_specs=pl.BlockSpec((tm, tn), lambda i, j: (i, j)),
        out_shape=jax.ShapeDtypeStruct((M, N), BF16),
        compiler_params=pltpu.CompilerParams(
            dimension_semantics=("arbitrary", "arbitrary"),
            vmem_limit_bytes=_vmem_limit([_nbytes((tm, K), BF16), 2 * _nbytes((K, tn), BF16),
                                          _nbytes((tm, tn), BF16)], 4 * _nbytes((tm, tn), F32))),
        name="ffn_up",
    )(h, wg, wu)


def _pad_cols(w, n):
    return jnp.pad(w, ((0, 0), (0, n - w.shape[1])))


FFN_PAD = 512


def _swiglu(h, layer, w_gate, w_up, w_down, hosted=None):
    F, D = w_gate.shape[2], w_gate.shape[1]
    if hosted is None:
        (a,) = _mm_ws_call(h, [w_gate, w_up], layer, 0, F, _ep_swiglu, [(F, None, BF16)], tn=256, name="ffn_up")
        (y,) = _mm_call(a, w_down[layer].astype(BF16), _ep_bf16, [(D, None, BF16)],
                        tk=F // 2 if F % (2 * LANES) == 0 else F, name="ffn_down")
        return y
    wg, wu, wd = hosted
    Fp = wd.shape[0]
    a = _ffn_up_call(h, wg, wu, tn=FFN_PAD)
    (y,) = _mm_call(a, wd, _ep_bf16, [(D, None, BF16)], tn=1024,
                    tk=Fp // 4 if Fp % (4 * 2 * LANES) == 0 else Fp, name="ffn_down")
    return y


def kernel(x, c, positions, ada_w, ada_b, norm_g, ffn_w_gate, ffn_w_up, ffn_w_down, hg_w_in, hg_lb_logits, hg_norm_g, hg_w_out, mla_w_dq, mla_q_norm_g, mla_w_uq, mla_w_o, kv_norm_in_g, kv_w_dkv, kv_norm_g, kv_w_ukv):
    B, S, D = x.shape
    assert B == 1
    depth = ada_w.shape[0]
    n_a = hg_w_in.shape[0]
    mla_heads = kv_w_ukv.shape[1] // (NOPE_DIM + V_HEAD_DIM)
    fdim = hg_lb_logits.shape[1]

    xs = x.reshape(S, D)
    mod = _ada_call(c, ada_w, ada_b).reshape(depth, 6, D)
    cos_t, sin_t = _rope_call(positions)
    lower_bounds = jnp.cumsum(jax.nn.softmax(hg_lb_logits.astype(F32), axis=0), axis=0)

    ffn_dim = ffn_w_gate.shape[2]
    ffn_pad = -(-ffn_dim // FFN_PAD) * FFN_PAD

    h = _prenorm_call(xs, norm_g[0, 0], mod[0, 1], mod[0, 0])
    kv = kpe = None
    for l in range(depth):
        sh_m, sc_m, g_m, sh_f, sc_f, g_f = (mod[l, i] for i in range(6))
        hosted = None
        if l < n_a:
            lb = lower_bounds[l].reshape(1, fdim)
            w_in = [hg_w_in]
            (q,) = _mm_ws_call(h, w_in, l, 0, fdim, _ep_silu, [(fdim, None, BF16)], name="hg_q")
            lf, k = _mm_ws_call(h, w_in, l, fdim, fdim, _ep_forget, [(fdim, None, F32), (fdim, None, BF16)],
                                extras=[(lb, "col")], name="hg_f")
            (v,) = _mm_ws_call(h, w_in, l, 2 * fdim, D, _ep_bf16, [(D, None, BF16)], name="hg_i")
            (gs,) = _mm_ws_call(h, w_in, l, 2 * fdim + D, D, _ep_silu, [(D, None, BF16)], name="hg_g")
            o = _hgrn_call(q, k, v, lf, gs, hg_norm_g[l])
            (y,) = _mm_ws_call(o, [hg_w_out], l, 0, D, _ep_bf16, [(D, None, BF16)], name="hg_out")
        else:
            j = l - n_a
            q_lora = mla_w_dq.shape[2]
            (qd,) = _mm_call(h, mla_w_dq[j].astype(BF16), _ep_rms, [(q_lora, q_lora, BF16)],
                             extras=[(mla_q_norm_g[j].reshape(1, q_lora), "col")], tn=q_lora, name="mla_dq")
            w_uq = _uq_pad_call(mla_w_uq, j, mla_heads)
            (q,) = _mm_call(qd, w_uq, _ep_q_rope, [(w_uq.shape[1], None, BF16)],
                            extras=[(cos_t, "row"), (sin_t, "row")], tn=2048, name="mla_uq")
            o, (wg, wu, wd, wo) = _attn_call(q, kv, kpe, side=[
                (ffn_w_gate, l, D, ffn_pad), (ffn_w_up, l, D, ffn_pad),
                (ffn_w_down, l, ffn_pad, D), (mla_w_o, j, mla_w_o.shape[1], D)])
            hosted = (wg, wu, wd)
            (y,) = _mm_call(o, wo, _ep_bf16, [(D, None, BF16)], name="mla_o")
        xs, h = _resid_call(xs, y, g_m, norm_g[l, 1], [(norm_g[l, 2], sc_f, sh_f)])
        y = _swiglu(h, l, ffn_w_gate, ffn_w_up, ffn_w_down, hosted)
        norms = []
        if l + 1 < depth:
            norms.append((norm_g[l + 1, 0], mod[l + 1, 1], mod[l + 1, 0]))
        if l == n_a - 1:
            norms.append((kv_norm_in_g,))
        res = _resid_call(xs, y, g_f, norm_g[l, 3], norms)
        xs = res[0]
        if l + 1 < depth:
            h = res[1]
        if l == n_a - 1:
            src = res[-1]
            n_dkv = KV_LORA + LANES
            ckvn, kpe = _mm_call(src, _pad_cols(kv_w_dkv, n_dkv).astype(BF16), _ep_dkv,
                                 [(KV_LORA, KV_LORA, BF16), (LANES, LANES, BF16)],
                                 extras=[(kv_norm_g.reshape(1, KV_LORA), "full"), (cos_t, "row"), (sin_t, "row")],
                                 tn=n_dkv, name="mla_dkv")
            (kv,) = _mm_call(ckvn, kv_w_ukv.astype(BF16), _ep_bf16, [(kv_w_ukv.shape[1], None, BF16)],
                             tm=2048, tn=1024, name="mla_ukv")
    return xs.reshape(B, S, D)
```

```python
import functools
import math

import jax
import jax.numpy as jnp
from jax import lax
from jax.experimental import pallas as pl
from jax.experimental.pallas import tpu as pltpu

F32 = jnp.float32
BF16 = jnp.bfloat16

HG_HEAD_DIM = 128
NOPE_DIM = 128
ROPE_DIM = 64
V_HEAD_DIM = 128
QK_HEAD_DIM = NOPE_DIM + ROPE_DIM
KV_LORA = 512
ROPE_THETA = 10000.0
NORM_EPS = 1e-6
ATTN_SCALE = 1.0 / math.sqrt(QK_HEAD_DIM)
Q_SCALE_LOG2 = ATTN_SCALE * math.log2(math.e)

LANES = 128
SUBLANES = 8
VMEM_BYTES_V7X = 64 * 1024 * 1024
VMEM_LIMIT_CAP = VMEM_BYTES_V7X - 8 * 1024 * 1024
VMEM_LIMIT_FLOOR = 32 * 1024 * 1024

Q_HEAD_PAD = 2 * LANES
HG_CHUNK = 128
HG_LEVELS = 7


def _vmem_limit(block_bytes, temp_bytes=0):
    est = 2 * sum(block_bytes) + temp_bytes + (4 << 20)
    return int(min(max(est, VMEM_LIMIT_FLOOR), VMEM_LIMIT_CAP))


def _nbytes(shape, dtype):
    return math.prod(shape) * jnp.dtype(dtype).itemsize


def _tile(dim, pref):
    if dim <= pref:
        return dim
    t = (pref // LANES) * LANES
    while t >= LANES:
        if dim % t == 0:
            return t
        t -= LANES
    return dim


def _sigmoid(x):
    return 1.0 / (1.0 + jnp.exp(-x))


def _silu(x):
    return x * _sigmoid(x)


def _rms(x, g):
    ms = jnp.mean(x * x, axis=-1, keepdims=True)
    return x * lax.rsqrt(ms + NORM_EPS) * g


def _ada_kernel(c_ref, w_ref, b_ref, o_ref, acc_ref, *, nk):
    k = pl.program_id(2)
    tk, tn = w_ref.shape[1], w_ref.shape[2]
    p = w_ref[0] * _silu(c_ref[...])
    part = p.reshape(tk // SUBLANES, SUBLANES, tn).sum(axis=0)

    @pl.when(k == 0)
    def _():
        acc_ref[...] = part

    @pl.when(k > 0)
    def _():
        acc_ref[...] += part

    @pl.when(k == nk - 1)
    def _():
        o_ref[0] = acc_ref[...].sum(axis=0, keepdims=True) + b_ref[0]


def _ada_call(c, ada_w, ada_b):
    L, D, N = ada_w.shape
    tk, tn = _tile(D, 1024), _tile(N, 2048)
    nk = D // tk
    c_col = c.reshape(D, 1)
    b3 = ada_b.reshape(L, 1, N)
    return pl.pallas_call(
        functools.partial(_ada_kernel, nk=nk),
        grid=(L, N // tn, nk),
        in_specs=[
            pl.BlockSpec((tk, 1), lambda l, j, k: (k, 0)),
            pl.BlockSpec((1, tk, tn), lambda l, j, k: (l, k, j)),
            pl.BlockSpec((1, 1, tn), lambda l, j, k: (l, 0, j)),
        ],
        out_specs=pl.BlockSpec((1, 1, tn), lambda l, j, k: (l, 0, j)),
        out_shape=jax.ShapeDtypeStruct((L, 1, N), F32),
        scratch_shapes=[pltpu.VMEM((SUBLANES, tn), F32)],
        compiler_params=pltpu.CompilerParams(
            dimension_semantics=("arbitrary", "arbitrary", "arbitrary"),
            vmem_limit_bytes=_vmem_limit([_nbytes((tk, tn), F32), _nbytes((tk, LANES), F32)],
                                         _nbytes((tk, tn), F32))),
        name="ada_gemv",
    )(c_col, ada_w, b3)


def _rope_kernel(pos_ref, inv_ref, cmask_ref, sgn_ref, cos_ref, sin_ref):
    ang = pos_ref[...].astype(F32) * inv_ref[...]
    cos_ref[...] = jnp.cos(ang) * cmask_ref[...]
    sin_ref[...] = jnp.sin(ang) * sgn_ref[...]


def _rope_call(positions):
    S = positions.shape[-1]
    half = ROPE_DIM // 2
    inv_freq = 1.0 / (ROPE_THETA ** (jnp.arange(0, ROPE_DIM, 2, dtype=F32) / ROPE_DIM))
    zeros = jnp.zeros((LANES - ROPE_DIM,), F32)
    inv = jnp.concatenate([inv_freq, inv_freq, zeros]).reshape(1, LANES)
    cmask = jnp.concatenate([jnp.ones((ROPE_DIM,), F32), zeros]).reshape(1, LANES)
    sgn = jnp.concatenate([-jnp.ones((half,), F32), jnp.ones((half,), F32), zeros]).reshape(1, LANES)
    tm = _tile(S, 1024)
    row = pl.BlockSpec((1, LANES), lambda i: (0, 0))
    out = pl.BlockSpec((tm, LANES), lambda i: (i, 0))
    return pl.pallas_call(
        _rope_kernel,
        grid=(S // tm,),
        in_specs=[pl.BlockSpec((tm, 1), lambda i: (i, 0)), row, row, row],
        out_specs=[out, out],
        out_shape=[jax.ShapeDtypeStruct((S, LANES), F32)] * 2,
        compiler_params=pltpu.CompilerParams(dimension_semantics=("arbitrary",)),
        name="rope_tables",
    )(positions.reshape(S, 1), inv, cmask, sgn)


def _rope_block(blk, cos_t, sin_t):
    half = ROPE_DIM // 2
    lane = lax.broadcasted_iota(jnp.int32, blk.shape, 1)
    swapped = jnp.where(lane < half, pltpu.roll(blk, LANES - half, 1), pltpu.roll(blk, half, 1))
    return blk * cos_t + swapped * sin_t


def _prenorm_kernel(x_ref, g_ref, sc_ref, sh_ref, o_ref):
    y = _rms(x_ref[...], g_ref[...])
    o_ref[...] = (y * (1.0 + sc_ref[...]) + sh_ref[...]).astype(o_ref.dtype)


def _prenorm_call(x, g, sc, sh):
    S, D = x.shape
    tm = _tile(S, 256)
    vec = pl.BlockSpec((1, D), lambda i: (0, 0))
    row = pl.BlockSpec((tm, D), lambda i: (i, 0))
    return pl.pallas_call(
        _prenorm_kernel,
        grid=(S // tm,),
        in_specs=[row, vec, vec, vec],
        out_specs=row,
        out_shape=jax.ShapeDtypeStruct((S, D), BF16),
        compiler_params=pltpu.CompilerParams(
            dimension_semantics=("arbitrary",),
            vmem_limit_bytes=_vmem_limit([_nbytes((tm, D), F32), _nbytes((tm, D), BF16)],
                                         3 * _nbytes((tm, D), F32))),
        name="prenorm",
    )(x, g.reshape(1, D), sc.reshape(1, D), sh.reshape(1, D))


def _resid_kernel(*refs, modulated):
    x_ref, y_ref, gate_ref, w_ref = refs[:4]
    n_out = len(modulated)
    n_par = sum(3 if m else 1 for m in modulated)
    par = refs[4:4 + n_par]
    xo_ref = refs[4 + n_par]
    outs = refs[5 + n_par:5 + n_par + n_out]
    xn = x_ref[...] + gate_ref[...] * _rms(y_ref[...].astype(F32), w_ref[...])
    xo_ref[...] = xn
    if n_out:
        inv = lax.rsqrt(jnp.mean(xn * xn, axis=-1, keepdims=True) + NORM_EPS)
        xh = xn * inv
        p = 0
        for m, o_ref in zip(modulated, outs):
            h = xh * par[p][...]
            if m:
                h = h * (1.0 + par[p + 1][...]) + par[p + 2][...]
            p += 3 if m else 1
            o_ref[...] = h.astype(o_ref.dtype)


def _resid_call(x, y, gate, w, norms):
    S, D = x.shape
    tm = _tile(S, 256)
    vec = pl.BlockSpec((1, D), lambda i: (0, 0))
    row = pl.BlockSpec((tm, D), lambda i: (i, 0))
    modulated = tuple(len(n) == 3 for n in norms)
    params = [p.reshape(1, D) for n in norms for p in n]
    n_out = len(norms)
    return pl.pallas_call(
        functools.partial(_resid_kernel, modulated=modulated),
        grid=(S // tm,),
        in_specs=[row, row, vec, vec] + [vec] * len(params),
        out_specs=[row] * (1 + n_out),
        out_shape=[jax.ShapeDtypeStruct((S, D), F32)] + [jax.ShapeDtypeStruct((S, D), BF16)] * n_out,
        compiler_params=pltpu.CompilerParams(
            dimension_semantics=("arbitrary",),
            vmem_limit_bytes=_vmem_limit([_nbytes((tm, D), F32)] * 3 + [_nbytes((tm, D), BF16)] * n_out,
                                         4 * _nbytes((tm, D), F32))),
        name="resid_norm",
    )(x, y, gate.reshape(1, D), w.reshape(1, D), *params)


def _mm_kernel(x_ref, w_ref, *rest, nk, n_extra, n_out, epilogue):
    extras = rest[:n_extra]
    outs = rest[n_extra:n_extra + n_out]
    w = w_ref[0] if len(w_ref.shape) == 3 else w_ref[...]
    prod = jnp.dot(x_ref[...], w, preferred_element_type=F32)
    if nk == 1:
        epilogue(prod, extras, outs)
        return
    acc_ref = rest[n_extra + n_out]
    k = pl.program_id(2)

    @pl.when(k == 0)
    def _():
        acc_ref[...] = prod

    @pl.when(k > 0)
    def _():
        acc_ref[...] += prod

    @pl.when(k == nk - 1)
    def _():
        epilogue(acc_ref[...], extras, outs)


def _mm_call(x, w, epilogue, outs, *, extras=(), layer=None, tm=1024, tn=512, tk=4096, name="matmul"):
    M, K = x.shape
    N = w.shape[-1]
    tm, tn, tk = _tile(M, tm), _tile(N, tn), _tile(K, tk)
    nk = K // tk
    if layer is None:
        w_spec = pl.BlockSpec((tk, tn), lambda i, j, k: (k, j))
    else:
        w_spec = pl.BlockSpec((1, tk, tn), lambda i, j, k: (layer, k, j))
    in_specs = [pl.BlockSpec((tm, tk), lambda i, j, k: (i, k)), w_spec]
    blocks = [_nbytes((tm, tk), x.dtype), _nbytes((tk, tn), w.dtype)]
    arrays = []
    for arr, kind in extras:
        if kind == "col":
            in_specs.append(pl.BlockSpec((1, tn), lambda i, j, k: (0, j)))
            blocks.append(_nbytes((SUBLANES, tn), arr.dtype))
        elif kind == "row":
            in_specs.append(pl.BlockSpec((tm, arr.shape[1]), lambda i, j, k: (i, 0)))
            blocks.append(_nbytes((tm, arr.shape[1]), arr.dtype))
        else:
            in_specs.append(pl.BlockSpec(arr.shape, lambda i, j, k: (0,) * arr.ndim))
            blocks.append(_nbytes(arr.shape, arr.dtype))
        arrays.append(arr)
    out_specs, out_shape = [], []
    for n_tot, n_blk, dt in outs:
        n_blk = tn if n_blk is None else n_blk
        out_specs.append(pl.BlockSpec((tm, n_blk), lambda i, j, k: (i, j)))
        out_shape.append(jax.ShapeDtypeStruct((M, n_tot), dt))
        blocks.append(_nbytes((tm, n_blk), dt))
    scratch = [pltpu.VMEM((tm, tn), F32)] if nk > 1 else []
    return pl.pallas_call(
        functools.partial(_mm_kernel, nk=nk, n_extra=len(arrays), n_out=len(outs), epilogue=epilogue),
        grid=(M // tm, N // tn, nk),
        in_specs=in_specs,
        out_specs=out_specs,
        out_shape=out_shape,
        scratch_shapes=scratch,
        compiler_params=pltpu.CompilerParams(
            dimension_semantics=("arbitrary", "arbitrary", "arbitrary"),
            vmem_limit_bytes=_vmem_limit(blocks, 2 * _nbytes((tm, tn), F32))),
        name=name,
    )(x, w, *arrays)


def _mm_ws_kernel(x_ref, *rest, n_w, n_extra, n_out, epilogue):
    w_refs = rest[:n_w]
    extras = rest[n_w:n_w + n_extra]
    outs = rest[n_w + n_extra:n_w + n_extra + n_out]
    wbf = rest[n_w + n_extra + n_out:]

    @pl.when(pl.program_id(1) == 0)
    def _():
        for w_ref, wb in zip(w_refs, wbf):
            wb[...] = w_ref[0].astype(BF16)

    x = x_ref[...]
    prods = [jnp.dot(x, wb[...], preferred_element_type=F32) for wb in wbf]
    epilogue(prods[0] if n_w == 1 else prods, extras, outs)


def _mm_ws_call(x, ws, layer, col0, n_cols, epilogue, outs, *, extras=(), tm=1024, tn=512, name="matmul_ws"):
    M, K = x.shape
    tm, tn = _tile(M, tm), _tile(n_cols, tn)
    assert col0 % tn == 0
    off = col0 // tn
    in_specs = [pl.BlockSpec((tm, K), lambda j, i: (i, 0))]
    in_specs += [pl.BlockSpec((1, K, tn), lambda j, i: (layer, 0, j + off))] * len(ws)
    blocks = [_nbytes((tm, K), x.dtype)] + [_nbytes((K, tn), F32)] * len(ws)
    arrays = []
    for arr, kind in extras:
        if kind == "col":
            in_specs.append(pl.BlockSpec((1, tn), lambda j, i: (0, j)))
            blocks.append(_nbytes((SUBLANES, tn), arr.dtype))
        elif kind == "row":
            in_specs.append(pl.BlockSpec((tm, arr.shape[1]), lambda j, i: (i, 0)))
            blocks.append(_nbytes((tm, arr.shape[1]), arr.dtype))
        else:
            in_specs.append(pl.BlockSpec(arr.shape, lambda j, i: (0,) * arr.ndim))
            blocks.append(_nbytes(arr.shape, arr.dtype))
        arrays.append(arr)
    out_specs, out_shape = [], []
    for n_tot, n_blk, dt in outs:
        n_blk = tn if n_blk is None else n_blk
        out_specs.append(pl.BlockSpec((tm, n_blk), lambda j, i: (i, j)))
        out_shape.append(jax.ShapeDtypeStruct((M, n_tot), dt))
        blocks.append(_nbytes((tm, n_blk), dt))
    scratch_bytes = len(ws) * _nbytes((K, tn), BF16)
    return pl.pallas_call(
        functools.partial(_mm_ws_kernel, n_w=len(ws), n_extra=len(arrays), n_out=len(outs), epilogue=epilogue),
        grid=(n_cols // tn, M // tm),
        in_specs=in_specs,
        out_specs=out_specs,
        out_shape=out_shape,
        scratch_shapes=[pltpu.VMEM((K, tn), BF16)] * len(ws),
        compiler_params=pltpu.CompilerParams(
            dimension_semantics=("arbitrary", "arbitrary"),
            vmem_limit_bytes=_vmem_limit(blocks, scratch_bytes + (len(ws) + 1) * _nbytes((tm, tn), F32))),
        name=name,
    )(x, *ws, *arrays)


def _ep_swiglu(acc, extras, outs):
    g, u = acc
    outs[0][...] = (_silu(g) * u).astype(BF16)


def _ep_bf16(acc, extras, outs):
    outs[0][...] = acc.astype(BF16)


def _ep_silu(acc, extras, outs):
    outs[0][...] = _silu(acc).astype(BF16)


def _ep_forget(acc, extras, outs):
    lb = extras[0][...]
    f = lb + (1.0 - lb) * _sigmoid(acc)
    outs[0][...] = jnp.log2(f)
    outs[1][...] = (1.0 - f).astype(BF16)


def _ep_rms(acc, extras, outs):
    outs[0][...] = _rms(acc, extras[0][...]).astype(BF16)


def _ep_q_rope(acc, extras, outs):
    cos_t, sin_t = extras[0][...], extras[1][...]
    for h in range(acc.shape[1] // Q_HEAD_PAD):
        lo = h * Q_HEAD_PAD
        outs[0][:, lo:lo + LANES] = (acc[:, lo:lo + LANES] * Q_SCALE_LOG2).astype(BF16)
        pe = _rope_block(acc[:, lo + LANES:lo + Q_HEAD_PAD], cos_t, sin_t)
        outs[0][:, lo + LANES:lo + Q_HEAD_PAD] = (pe * Q_SCALE_LOG2).astype(BF16)


def _uq_pad_kernel(w_ref, o_ref):
    w = w_ref[0]
    z = jnp.zeros((w.shape[0], Q_HEAD_PAD - QK_HEAD_DIM), F32)
    o_ref[...] = jnp.concatenate([w[:, :QK_HEAD_DIM], z, w[:, QK_HEAD_DIM:], z], axis=1).astype(BF16)


def _uq_pad_call(w_uq, layer, heads):
    L, Q, N = w_uq.shape
    assert heads % 2 == 0 and N == heads * QK_HEAD_DIM
    return pl.pallas_call(
        _uq_pad_kernel,
        grid=(heads // 2,),
        in_specs=[pl.BlockSpec((1, Q, 2 * QK_HEAD_DIM), lambda p: (layer, 0, p))],
        out_specs=pl.BlockSpec((Q, 2 * Q_HEAD_PAD), lambda p: (0, p)),
        out_shape=jax.ShapeDtypeStruct((Q, heads * Q_HEAD_PAD), BF16),
        compiler_params=pltpu.CompilerParams(dimension_semantics=("arbitrary",)),
        name="uq_pad",
    )(w_uq)


def _ep_dkv(acc, extras, outs):
    g, cos_t, sin_t = extras[0][...], extras[1][...], extras[2][...]
    outs[0][...] = _rms(acc[:, :KV_LORA], g).astype(BF16)
    outs[1][...] = _rope_block(acc[:, KV_LORA:KV_LORA + LANES], cos_t, sin_t).astype(BF16)


def _boundary_cumsum(b, m):
    T, W = b.shape
    if 2 * m >= SUBLANES:
        nb = T // (2 * m)
        b3 = b.reshape(nb, 2 * m, W)
        return jnp.broadcast_to(b3[:, m - 1:m, :], (nb, 2 * m, W)).reshape(T, W)
    b3 = b.reshape(T // SUBLANES, SUBLANES, W)
    sub = lax.broadcasted_iota(jnp.int32, b3.shape, 1)
    r = jnp.broadcast_to(b3[:, m - 1:m, :], b3.shape)
    for blk in range(1, SUBLANES // (2 * m)):
        row = blk * 2 * m + m - 1
        r = jnp.where(sub >= blk * 2 * m, jnp.broadcast_to(b3[:, row:row + 1, :], b3.shape), r)
    return r.reshape(T, W)


def _hgrn_kernel(q_ref, k_ref, v_ref, lf_ref, gs_ref, ng_ref, tril_ref, sgn_ref, rowsel_ref, pmask_ref,
                 *rest, n_chunks, nh, side_blocks):
    T = HG_CHUNK

    n_side = len(side_blocks)
    side_in = rest[:n_side]
    o_ref = rest[n_side]
    side_out = rest[n_side + 1:2 * n_side + 1]
    st_ref = rest[2 * n_side + 1]
    step = pl.program_id(0) * pl.num_programs(1) + pl.program_id(1)
    for w_ref, c_ref, (nb_in, nb_out) in zip(side_in, side_out, side_blocks):
        _side_cast(step, w_ref, c_ref, nb_in, nb_out)

    @pl.when(pl.program_id(1) == 0)
    def _():
        st_ref[...] = jnp.zeros_like(st_ref)

    nt = (((1,), (1,)), ((), ()))
    tn = (((0,), (0,)), ((), ()))
    heads = range(nh)
    cols = [slice(hh * HG_HEAD_DIM, (hh + 1) * HG_HEAD_DIM) for hh in heads]

    def chunk(c, carry):
        rows = pl.ds(pl.multiple_of(c * T, T), T)
        lf = [lf_ref[rows, cs] for cs in cols]
        b = []
        for hh in heads:
            hi = lf[hh].astype(BF16)
            r1 = lf[hh] - hi.astype(F32)
            mid = r1.astype(BF16)
            lo = (r1 - mid.astype(F32)).astype(BF16)
            parts = jnp.dot(tril_ref[...], jnp.concatenate([hi, mid, lo], axis=1),
                            preferred_element_type=F32)
            b.append(parts[:, :LANES] + parts[:, LANES:2 * LANES] + parts[:, 2 * LANES:])
        qb = [q_ref[rows, cs] for cs in cols]
        kb = [k_ref[rows, cs] for cs in cols]
        vb = [v_ref[rows, cs] for cs in cols]
        q = [t.astype(F32) for t in qb]
        k = [t.astype(F32) for t in kb]
        st = [st_ref[hh] for hh in heads]
        o = [lax.dot_general((q[hh] * jnp.exp2(b[hh])).astype(BF16), st[hh].astype(BF16), nt,
                             preferred_element_type=F32) for hh in heads]
        a = [jnp.where(pmask_ref[HG_LEVELS] > 0.5,
                       lax.dot_general(qb[hh], kb[hh], nt, preferred_element_type=F32), 0.0)
             for hh in heads]
        for lvl in range(HG_LEVELS):
            second = rowsel_ref[lvl] > 0.5
            for hh in heads:
                if lvl == 0:
                    arg = lf[hh] * rowsel_ref[0]
                else:
                    arg = (b[hh] - _boundary_cumsum(b[hh], 1 << lvl)) * sgn_ref[lvl]
                x = (jnp.where(second, q[hh], k[hh]) * jnp.exp2(arg)).astype(BF16)
                p = lax.dot_general(x, x, nt, preferred_element_type=F32)
                a[hh] = jnp.where(pmask_ref[lvl] > 0.5, p, a[hh])
        for hh in heads:
            o[hh] = o[hh] + jnp.dot(a[hh].astype(BF16), vb[hh], preferred_element_type=F32)
        for hh in heads:
            b_last = b[hh][T - 1:T, :]
            kt = (k[hh] * jnp.exp2(b_last - b[hh])).astype(BF16)
            st_ref[hh] = st[hh] * jnp.exp2(b_last) + lax.dot_general(vb[hh], kt, tn,
                                                                     preferred_element_type=F32)
        for hh in heads:
            y = _rms(o[hh], ng_ref[...]) * gs_ref[rows, cols[hh]].astype(F32)
            o_ref[rows, cols[hh]] = y.astype(o_ref.dtype)
        return carry

    lax.fori_loop(0, n_chunks, chunk, 0)


def _hgrn_masks():
    T = HG_CHUNK
    t = jnp.arange(T)
    tril = (t[:, None] >= t[None, :]).astype(BF16)
    rowsel, pmask = [], []
    for lvl in range(HG_LEVELS):
        m = 1 << lvl
        second = (t // m) % 2 == 1
        rowsel.append(jnp.broadcast_to(second[:, None], (T, HG_HEAD_DIM)))
        same = (t[:, None] // (2 * m)) == (t[None, :] // (2 * m))
        pmask.append(same & second[:, None] & (~second)[None, :])
    pmask.append(t[:, None] == t[None, :])
    rowsel = jnp.stack(rowsel).astype(F32)
    return tril, 2.0 * rowsel - 1.0, rowsel, jnp.stack(pmask).astype(F32)


def _hgrn_call(q, k, v, lf2, gs, norm_g, side=(), *, t_blk=512, nh=8):
    S, D = q.shape
    H = D // HG_HEAD_DIM
    t_blk = _tile(S, t_blk)
    nh = math.gcd(nh, H)
    assert t_blk % HG_CHUNK == 0
    tril, sgn, rowsel, pmask = _hgrn_masks()
    w_blk = nh * HG_HEAD_DIM
    n_s = S // t_blk
    seq = pl.BlockSpec((t_blk, w_blk), lambda h, s: (s, h))
    plans = [_side_plan(w, layer, (H // nh) * n_s, rows_out, cols_out, lambda h, s: h * n_s + s)
             for w, layer, rows_out, cols_out in side]

    def const(a):
        return pl.BlockSpec(a.shape, lambda h, s: (0,) * a.ndim)

    ng = norm_g.reshape(1, HG_HEAD_DIM)
    res = pl.pallas_call(
        functools.partial(_hgrn_kernel, n_chunks=t_blk // HG_CHUNK, nh=nh,
                          side_blocks=tuple(p[4] for p in plans)),
        grid=(H // nh, n_s),
        in_specs=[seq, seq, seq, seq, seq, const(ng), const(tril), const(sgn), const(rowsel), const(pmask)]
        + [p[1] for p in plans],
        out_specs=[seq] + [p[2] for p in plans],
        out_shape=[jax.ShapeDtypeStruct((S, D), BF16)] + [p[3] for p in plans],
        scratch_shapes=[pltpu.VMEM((nh, HG_HEAD_DIM, HG_HEAD_DIM), F32)],
        compiler_params=pltpu.CompilerParams(
            dimension_semantics=("arbitrary", "arbitrary"),
            vmem_limit_bytes=_vmem_limit([5 * _nbytes((t_blk, w_blk), BF16), _nbytes((t_blk, w_blk), F32)]
                                         + [p[5] for p in plans],
                                         64 * nh * _nbytes((HG_CHUNK, HG_HEAD_DIM), F32))),
        name="hgrn2_recurrence",
    )(q, k, v, lf2, gs, ng, tril, sgn, rowsel, pmask, *[p[0] for p in plans])
    return res[0], list(res[1:])


def _side_cast(step, w_ref, o_ref, nb_in, nb_out):
    val = w_ref[0]
    extra = o_ref.shape[1] - val.shape[1]
    if extra:
        val = jnp.concatenate([val, jnp.zeros((val.shape[0], extra), F32)], axis=1)
    val = val.astype(BF16)
    if nb_out != nb_in:
        val = jnp.where(jnp.minimum(step, nb_out - 1) < nb_in, val, jnp.zeros_like(val))
    o_ref[...] = val


def _side_plan(w, layer, steps, rows_out, cols_out, index_of_step):
    L, K, N = w.shape
    rb = next(r for r in range(16, rows_out + 1, 16)
              if K % r == 0 and rows_out % r == 0 and rows_out // r <= steps)
    nb_in, nb_out = K // rb, rows_out // rb
    in_spec = pl.BlockSpec((1, rb, N), lambda *g: (layer, jnp.minimum(index_of_step(*g), nb_in - 1), 0))
    out_spec = pl.BlockSpec((rb, cols_out), lambda *g: (jnp.minimum(index_of_step(*g), nb_out - 1), 0))
    out_shape = jax.ShapeDtypeStruct((rows_out, cols_out), BF16)
    nbytes = _nbytes((rb, N), F32) + _nbytes((rb, cols_out), BF16)
    return w, in_spec, out_spec, out_shape, (nb_in, nb_out), nbytes


def _attn_kernel(q_ref, kn_ref, v_ref, kpe_ref, *rest, tq, tk, tk_diag, nsub, side_blocks):
    n_side = len(side_blocks)
    side_in = rest[:n_side]
    o_ref = rest[n_side]
    side_out = rest[n_side + 1:2 * n_side + 1]
    m_ref, acc_ref = rest[2 * n_side + 1:]
    qi = pl.program_id(1)
    step = pl.program_id(0) * pl.num_programs(1) + qi
    tb = tq * nsub
    kv_per_block = tb // tk
    nt = (((1,), (1,)), ((), ()))
    m_ref[...] = jnp.full_like(m_ref, -jnp.inf)
    acc_ref[...] = jnp.zeros_like(acc_ref)

    def tile_update(a, kc, ve, mask_off):
        q = q_ref[a * tq:(a + 1) * tq, :]
        s = lax.dot_general(q, kc, nt, preferred_element_type=F32)
        if mask_off is not None:
            r = lax.broadcasted_iota(jnp.int32, s.shape, 0)
            c = lax.broadcasted_iota(jnp.int32, s.shape, 1)
            s = jnp.where(r + mask_off >= c, s, -jnp.inf)
        m_old = m_ref[a]
        m_new = jnp.maximum(m_old, s.max(axis=-1, keepdims=True))
        alpha = jnp.exp2(m_old - m_new)
        p = jnp.concatenate([jnp.exp2(s[:, cb * LANES:(cb + 1) * LANES] - m_new)
                             for cb in range(s.shape[1] // LANES)], axis=1).astype(BF16)
        pv = jnp.dot(p, ve, preferred_element_type=F32)
        acc_ref[a] = jnp.concatenate([alpha, alpha], axis=1) * acc_ref[a] + pv
        m_ref[a] = m_new

    def load_kv(start, size):
        rows = pl.ds(pl.multiple_of(start, size), size)
        kc = jnp.concatenate([kn_ref[rows, :], kpe_ref[rows, :]], axis=1)
        ve = jnp.concatenate([v_ref[rows, :], jnp.ones((size, LANES), BF16)], axis=1)
        return kc, ve

    def body(j, carry):
        kc, ve = load_kv(j * tk, tk)
        for a in range(nsub):
            tile_update(a, kc, ve, None)
        return carry

    lax.fori_loop(0, qi * kv_per_block, body, 0)
    for b in range(tb // tk_diag):
        kc, ve = load_kv(qi * tb + b * tk_diag, tk_diag)
        for a in range(nsub):
            if b * tk_diag > (a + 1) * tq - 1:
                continue
            unmasked = (b + 1) * tk_diag - 1 <= a * tq
            tile_update(a, kc, ve, None if unmasked else a * tq - b * tk_diag)
    for w_ref, c_ref, (nb_in, nb_out) in zip(side_in, side_out, side_blocks):
        _side_cast(step, w_ref, c_ref, nb_in, nb_out)
    for a in range(nsub):
        acc = acc_ref[a]
        o_ref[a * tq:(a + 1) * tq, :] = (acc[:, :V_HEAD_DIM] / acc[:, V_HEAD_DIM:]).astype(o_ref.dtype)


def _attn_call(q, kv, kpe, side=(), *, tq=256, tk=2048, tk_diag=256, nsub=8):
    S = q.shape[0]
    H = q.shape[1] // Q_HEAD_PAD
    nsub = min(nsub, S // tq)
    tb = tq * nsub
    tk = _tile(tb, tk)
    assert S % tb == 0 and tb % tk == 0 and tb % tk_diag == 0
    n_i = S // tb
    plans = [_side_plan(w, layer, H * n_i, rows_out, cols_out, lambda h, i: h * n_i + i)
             for w, layer, rows_out, cols_out in side]
    out_row = pl.BlockSpec((tb, V_HEAD_DIM), lambda h, i: (i, h))
    res = pl.pallas_call(
        functools.partial(_attn_kernel, tq=tq, tk=tk, tk_diag=tk_diag, nsub=nsub,
                          side_blocks=tuple(p[4] for p in plans)),
        grid=(H, n_i),
        in_specs=[
            pl.BlockSpec((tb, Q_HEAD_PAD), lambda h, i: (i, h)),
            pl.BlockSpec((S, NOPE_DIM), lambda h, i: (0, 2 * h)),
            pl.BlockSpec((S, V_HEAD_DIM), lambda h, i: (0, 2 * h + 1)),
            pl.BlockSpec((S, LANES), lambda h, i: (0, 0)),
        ] + [p[1] for p in plans],
        out_specs=[out_row] + [p[2] for p in plans],
        out_shape=[jax.ShapeDtypeStruct((S, H * V_HEAD_DIM), BF16)] + [p[3] for p in plans],
        scratch_shapes=[pltpu.VMEM((nsub, tq, LANES), F32), pltpu.VMEM((nsub, tq, 2 * LANES), F32)],
        compiler_params=pltpu.CompilerParams(
            dimension_semantics=("arbitrary", "arbitrary"),
            vmem_limit_bytes=_vmem_limit([3 * _nbytes((S, LANES), BF16), _nbytes((tb, Q_HEAD_PAD), BF16),
                                          _nbytes((tb, V_HEAD_DIM), BF16)] + [p[5] for p in plans],
                                         2 * nsub * _nbytes((tq, tk), F32))),
        name="mla_attention",
    )(q, kv, kv, kpe, *[p[0] for p in plans])
    return res[0], list(res[1:])


def _ffn_up_kernel(x_ref, wg_ref, wu_ref, o_ref):
    x = x_ref[...]
    g = jnp.dot(x, wg_ref[...], preferred_element_type=F32)
    u = jnp.dot(x, wu_ref[...], preferred_element_type=F32)
    o_ref[...] = (_silu(g) * u).astype(o_ref.dtype)


def _ffn_up_call(h, wg, wu, *, tm=1024, tn=512):
    M, K = h.shape
    N = wg.shape[1]
    tm, tn = _tile(M, tm), _tile(N, tn)
    wspec = pl.BlockSpec((K, tn), lambda i, j: (0, j))
    return pl.pallas_call(
        _ffn_up_kernel,
        grid=(M // tm, N // tn),
        in_specs=[pl.BlockSpec((tm, K), lambda i, j: (i, 0)), wspec, wspec],
        out_specs=pl.BlockSpec((tm, tn), lambda i, j: (i, j)),
        out_shape=jax.ShapeDtypeStruct((M, N), BF16),
        compiler_params=pltpu.CompilerParams(
            dimension_semantics=("arbitrary", "arbitrary"),
            vmem_limit_bytes=_vmem_limit([_nbytes((tm, K), BF16), 2 * _nbytes((K, tn), BF16),
                                          _nbytes((tm, tn), BF16)], 4 * _nbytes((tm, tn), F32))),
        name="ffn_up",
    )(h, wg, wu)


def _pad_cols(w, n):
    return jnp.pad(w, ((0, 0), (0, n - w.shape[1])))


FFN_PAD = 512


def _swiglu(h, layer, w_gate, w_up, w_down, hosted=None, down_bf16=None):
    F, D = w_gate.shape[2], w_gate.shape[1]
    if hosted is None:
        (a,) = _mm_ws_call(h, [w_gate, w_up], layer, 0, F, _ep_swiglu, [(F, None, BF16)], tn=256, name="ffn_up")
        wd = w_down[layer].astype(BF16) if down_bf16 is None else down_bf16
        (y,) = _mm_call(a, wd, _ep_bf16, [(D, None, BF16)],
                        tk=F // 2 if F % (2 * LANES) == 0 else F, name="ffn_down")
        return y
    wg, wu, wd = hosted
    Fp = wd.shape[0]
    a = _ffn_up_call(h, wg, wu, tn=FFN_PAD)
    (y,) = _mm_call(a, wd, _ep_bf16, [(D, None, BF16)], tn=1024,
                    tk=Fp // 4 if Fp % (4 * 2 * LANES) == 0 else Fp, name="ffn_down")
    return y


def kernel(x, c, positions, ada_w, ada_b, norm_g, ffn_w_gate, ffn_w_up, ffn_w_down, hg_w_in, hg_lb_logits, hg_norm_g, hg_w_out, mla_w_dq, mla_q_norm_g, mla_w_uq, mla_w_o, kv_norm_in_g, kv_w_dkv, kv_norm_g, kv_w_ukv):
    B, S, D = x.shape
    assert B == 1
    depth = ada_w.shape[0]
    n_a = hg_w_in.shape[0]
    mla_heads = kv_w_ukv.shape[1] // (NOPE_DIM + V_HEAD_DIM)
    fdim = hg_lb_logits.shape[1]

    xs = x.reshape(S, D)
    mod = _ada_call(c, ada_w, ada_b).reshape(depth, 6, D)
    cos_t, sin_t = _rope_call(positions)
    lower_bounds = jnp.cumsum(jax.nn.softmax(hg_lb_logits.astype(F32), axis=0), axis=0)

    ffn_dim = ffn_w_gate.shape[2]
    ffn_pad = -(-ffn_dim // FFN_PAD) * FFN_PAD

    h = _prenorm_call(xs, norm_g[0, 0], mod[0, 1], mod[0, 0])
    kv = kpe = None
    n_dkv = KV_LORA + LANES
    w_dkv_bf = w_ukv_bf = w_dq_bf = None
    for l in range(depth):
        down_bf16 = None
        sh_m, sc_m, g_m, sh_f, sc_f, g_f = (mod[l, i] for i in range(6))
        hosted = None
        if l < n_a:
            lb = lower_bounds[l].reshape(1, fdim)
            w_in = [hg_w_in]
            (q,) = _mm_ws_call(h, w_in, l, 0, fdim, _ep_silu, [(fdim, None, BF16)], name="hg_q")
            lf, k = _mm_ws_call(h, w_in, l, fdim, fdim, _ep_forget, [(fdim, None, F32), (fdim, None, BF16)],
                                extras=[(lb, "col")], name="hg_f")
            (v,) = _mm_ws_call(h, w_in, l, 2 * fdim, D, _ep_bf16, [(D, None, BF16)], name="hg_i")
            (gs,) = _mm_ws_call(h, w_in, l, 2 * fdim + D, D, _ep_silu, [(D, None, BF16)], name="hg_g")
            side = [(ffn_w_down, l, ffn_dim, D)]
            host_mla = l == n_a - 1 and depth > n_a
            if host_mla:
                side += [(kv_w_dkv[None], 0, D, n_dkv), (kv_w_ukv[None], 0, KV_LORA, kv_w_ukv.shape[1]),
                         (mla_w_dq, 0, D, mla_w_dq.shape[2])]
            o, casts = _hgrn_call(q, k, v, lf, gs, hg_norm_g[l], side)
            down_bf16 = casts[0]
            if host_mla:
                w_dkv_bf, w_ukv_bf, w_dq_bf = casts[1:]
            (y,) = _mm_ws_call(o, [hg_w_out], l, 0, D, _ep_bf16, [(D, None, BF16)], name="hg_out")
        else:
            j = l - n_a
            q_lora = mla_w_dq.shape[2]
            w_dq = w_dq_bf if (j == 0 and w_dq_bf is not None) else mla_w_dq[j].astype(BF16)
            (qd,) = _mm_call(h, w_dq, _ep_rms, [(q_lora, q_lora, BF16)],
                             extras=[(mla_q_norm_g[j].reshape(1, q_lora), "col")], tn=q_lora, name="mla_dq")
            w_uq = _uq_pad_call(mla_w_uq, j, mla_heads)
            (q,) = _mm_call(qd, w_uq, _ep_q_rope, [(w_uq.shape[1], None, BF16)],
                            extras=[(cos_t, "row"), (sin_t, "row")], tn=2048, name="mla_uq")
            o, (wg, wu, wd, wo) = _attn_call(q, kv, kpe, side=[
                (ffn_w_gate, l, D, ffn_pad), (ffn_w_up, l, D, ffn_pad),
                (ffn_w_down, l, ffn_pad, D), (mla_w_o, j, mla_w_o.shape[1], D)])
            hosted = (wg, wu, wd)
            (y,) = _mm_call(o, wo, _ep_bf16, [(D, None, BF16)], name="mla_o")
        xs, h = _resid_call(xs, y, g_m, norm_g[l, 1], [(norm_g[l, 2], sc_f, sh_f)])
        y = _swiglu(h, l, ffn_w_gate, ffn_w_up, ffn_w_down, hosted, down_bf16)
        norms = []
        if l + 1 < depth:
            norms.append((norm_g[l + 1, 0], mod[l + 1, 1], mod[l + 1, 0]))
        if l == n_a - 1:
            norms.append((kv_norm_in_g,))
        res = _resid_call(xs, y, g_f, norm_g[l, 3], norms)
        xs = res[0]
        if l + 1 < depth:
            h = res[1]
        if l == n_a - 1:
            src = res[-1]
            if w_dkv_bf is None:
                w_dkv_bf, w_ukv_bf = _pad_cols(kv_w_dkv, n_dkv).astype(BF16), kv_w_ukv.astype(BF16)
            ckvn, kpe = _mm_call(src, w_dkv_bf, _ep_dkv,
                                 [(KV_LORA, KV_LORA, BF16), (LANES, LANES, BF16)],
                                 extras=[(kv_norm_g.reshape(1, KV_LORA), "full"), (cos_t, "row"), (sin_t, "row")],
                                 tn=n_dkv, name="mla_dkv")
            (kv,) = _mm_call(ckvn, w_ukv_bf, _ep_bf16, [(kv_w_ukv.shape[1], None, BF16)],
                             tm=2048, tn=1024, name="mla_ukv")
    return xs.reshape(B, S, D)
```

```python
import functools
import math

import jax
import jax.numpy as jnp
from jax import lax
from jax.experimental import pallas as pl
from jax.experimental.pallas import tpu as pltpu

F32 = jnp.float32
BF16 = jnp.bfloat16

HG_HEAD_DIM = 128
NOPE_DIM = 128
ROPE_DIM = 64
V_HEAD_DIM = 128
QK_HEAD_DIM = NOPE_DIM + ROPE_DIM
KV_LORA = 512
ROPE_THETA = 10000.0
NORM_EPS = 1e-6
ATTN_SCALE = 1.0 / math.sqrt(QK_HEAD_DIM)
Q_SCALE_LOG2 = ATTN_SCALE * math.log2(math.e)

LANES = 128
SUBLANES = 8
VMEM_BYTES_V7X = 64 * 1024 * 1024
VMEM_LIMIT_CAP = VMEM_BYTES_V7X - 8 * 1024 * 1024
VMEM_LIMIT_FLOOR = 32 * 1024 * 1024

Q_HEAD_PAD = 2 * LANES
HG_CHUNK = 128
HG_LEVELS = 7


def _vmem_limit(block_bytes, temp_bytes=0):
    est = 2 * sum(block_bytes) + temp_bytes + (4 << 20)
    return int(min(max(est, VMEM_LIMIT_FLOOR), VMEM_LIMIT_CAP))


def _nbytes(shape, dtype):
    return math.prod(shape) * jnp.dtype(dtype).itemsize


def _tile(dim, pref):
    if dim <= pref:
        return dim
    t = (pref // LANES) * LANES
    while t >= LANES:
        if dim % t == 0:
            return t
        t -= LANES
    return dim


def _sigmoid(x):
    return 1.0 / (1.0 + jnp.exp(-x))


def _silu(x):
    return x * _sigmoid(x)


def _rms(x, g):
    ms = jnp.mean(x * x, axis=-1, keepdims=True)
    return x * lax.rsqrt(ms + NORM_EPS) * g


def _ada_kernel(c_ref, w_ref, b_ref, o_ref, acc_ref, *, nk):
    k = pl.program_id(2)
    tk, tn = w_ref.shape[1], w_ref.shape[2]
    p = w_ref[0] * _silu(c_ref[...])
    part = p.reshape(tk // SUBLANES, SUBLANES, tn).sum(axis=0)

    @pl.when(k == 0)
    def _():
        acc_ref[...] = part

    @pl.when(k > 0)
    def _():
        acc_ref[...] += part

    @pl.when(k == nk - 1)
    def _():
        o_ref[0] = acc_ref[...].sum(axis=0, keepdims=True) + b_ref[0]


def _ada_call(c, ada_w, ada_b):
    L, D, N = ada_w.shape
    tk, tn = _tile(D, 1024), _tile(N, 2048)
    nk = D // tk
    c_col = c.reshape(D, 1)
    b3 = ada_b.reshape(L, 1, N)
    return pl.pallas_call(
        functools.partial(_ada_kernel, nk=nk),
        grid=(L, N // tn, nk),
        in_specs=[
            pl.BlockSpec((tk, 1), lambda l, j, k: (k, 0)),
            pl.BlockSpec((1, tk, tn), lambda l, j, k: (l, k, j)),
            pl.BlockSpec((1, 1, tn), lambda l, j, k: (l, 0, j)),
        ],
        out_specs=pl.BlockSpec((1, 1, tn), lambda l, j, k: (l, 0, j)),
        out_shape=jax.ShapeDtypeStruct((L, 1, N), F32),
        scratch_shapes=[pltpu.VMEM((SUBLANES, tn), F32)],
        compiler_params=pltpu.CompilerParams(
            dimension_semantics=("arbitrary", "arbitrary", "arbitrary"),
            vmem_limit_bytes=_vmem_limit([_nbytes((tk, tn), F32), _nbytes((tk, LANES), F32)],
                                         _nbytes((tk, tn), F32))),
        name="ada_gemv",
    )(c_col, ada_w, b3)


def _rope_kernel(pos_ref, inv_ref, cmask_ref, sgn_ref, cos_ref, sin_ref):
    ang = pos_ref[...].astype(F32) * inv_ref[...]
    cos_ref[...] = jnp.cos(ang) * cmask_ref[...]
    sin_ref[...] = jnp.sin(ang) * sgn_ref[...]


def _rope_call(positions):
    S = positions.shape[-1]
    half = ROPE_DIM // 2
    inv_freq = 1.0 / (ROPE_THETA ** (jnp.arange(0, ROPE_DIM, 2, dtype=F32) / ROPE_DIM))
    zeros = jnp.zeros((LANES - ROPE_DIM,), F32)
    inv = jnp.concatenate([inv_freq, inv_freq, zeros]).reshape(1, LANES)
    cmask = jnp.concatenate([jnp.ones((ROPE_DIM,), F32), zeros]).reshape(1, LANES)
    sgn = jnp.concatenate([-jnp.ones((half,), F32), jnp.ones((half,), F32), zeros]).reshape(1, LANES)
    tm = _tile(S, 1024)
    row = pl.BlockSpec((1, LANES), lambda i: (0, 0))
    out = pl.BlockSpec((tm, LANES), lambda i: (i, 0))
    return pl.pallas_call(
        _rope_kernel,
        grid=(S // tm,),
        in_specs=[pl.BlockSpec((tm, 1), lambda i: (i, 0)), row, row, row],
        out_specs=[out, out],
        out_shape=[jax.ShapeDtypeStruct((S, LANES), F32)] * 2,
        compiler_params=pltpu.CompilerParams(dimension_semantics=("arbitrary",)),
        name="rope_tables",
    )(positions.reshape(S, 1), inv, cmask, sgn)


def _rope_block(blk, cos_t, sin_t):
    half = ROPE_DIM // 2
    lane = lax.broadcasted_iota(jnp.int32, blk.shape, 1)
    swapped = jnp.where(lane < half, pltpu.roll(blk, LANES - half, 1), pltpu.roll(blk, half, 1))
    return blk * cos_t + swapped * sin_t


def _prenorm_kernel(x_ref, g_ref, sc_ref, sh_ref, o_ref):
    y = _rms(x_ref[...], g_ref[...])
    o_ref[...] = (y * (1.0 + sc_ref[...]) + sh_ref[...]).astype(o_ref.dtype)


def _prenorm_call(x, g, sc, sh):
    S, D = x.shape
    tm = _tile(S, 256)
    vec = pl.BlockSpec((1, D), lambda i: (0, 0))
    row = pl.BlockSpec((tm, D), lambda i: (i, 0))
    return pl.pallas_call(
        _prenorm_kernel,
        grid=(S // tm,),
        in_specs=[row, vec, vec, vec],
        out_specs=row,
        out_shape=jax.ShapeDtypeStruct((S, D), BF16),
        compiler_params=pltpu.CompilerParams(
            dimension_semantics=("arbitrary",),
            vmem_limit_bytes=_vmem_limit([_nbytes((tm, D), F32), _nbytes((tm, D), BF16)],
                                         3 * _nbytes((tm, D), F32))),
        name="prenorm",
    )(x, g.reshape(1, D), sc.reshape(1, D), sh.reshape(1, D))


def _resid_kernel(*refs, modulated):
    x_ref, y_ref, gate_ref, w_ref = refs[:4]
    n_out = len(modulated)
    n_par = sum(3 if m else 1 for m in modulated)
    par = refs[4:4 + n_par]
    xo_ref = refs[4 + n_par]
    outs = refs[5 + n_par:5 + n_par + n_out]
    xn = x_ref[...] + gate_ref[...] * _rms(y_ref[...].astype(F32), w_ref[...])
    xo_ref[...] = xn
    if n_out:
        inv = lax.rsqrt(jnp.mean(xn * xn, axis=-1, keepdims=True) + NORM_EPS)
        xh = xn * inv
        p = 0
        for m, o_ref in zip(modulated, outs):
            h = xh * par[p][...]
            if m:
                h = h * (1.0 + par[p + 1][...]) + par[p + 2][...]
            p += 3 if m else 1
            o_ref[...] = h.astype(o_ref.dtype)


def _resid_call(x, y, gate, w, norms):
    S, D = x.shape
    tm = _tile(S, 256)
    vec = pl.BlockSpec((1, D), lambda i: (0, 0))
    row = pl.BlockSpec((tm, D), lambda i: (i, 0))
    modulated = tuple(len(n) == 3 for n in norms)
    params = [p.reshape(1, D) for n in norms for p in n]
    n_out = len(norms)
    return pl.pallas_call(
        functools.partial(_resid_kernel, modulated=modulated),
        grid=(S // tm,),
        in_specs=[row, row, vec, vec] + [vec] * len(params),
        out_specs=[row] * (1 + n_out),
        out_shape=[jax.ShapeDtypeStruct((S, D), F32)] + [jax.ShapeDtypeStruct((S, D), BF16)] * n_out,
        compiler_params=pltpu.CompilerParams(
            dimension_semantics=("arbitrary",),
            vmem_limit_bytes=_vmem_limit([_nbytes((tm, D), F32)] * 3 + [_nbytes((tm, D), BF16)] * n_out,
                                         4 * _nbytes((tm, D), F32))),
        name="resid_norm",
    )(x, y, gate.reshape(1, D), w.reshape(1, D), *params)


def _mm_kernel(x_ref, w_ref, *rest, nk, n_extra, n_out, epilogue):
    extras = rest[:n_extra]
    outs = rest[n_extra:n_extra + n_out]
    w = w_ref[0] if len(w_ref.shape) == 3 else w_ref[...]
    prod = jnp.dot(x_ref[...], w, preferred_element_type=F32)
    if nk == 1:
        epilogue(prod, extras, outs)
        return
    acc_ref = rest[n_extra + n_out]
    k = pl.program_id(2)

    @pl.when(k == 0)
    def _():
        acc_ref[...] = prod

    @pl.when(k > 0)
    def _():
        acc_ref[...] += prod

    @pl.when(k == nk - 1)
    def _():
        epilogue(acc_ref[...], extras, outs)


def _mm_call(x, w, epilogue, outs, *, extras=(), layer=None, tm=1024, tn=512, tk=4096, name="matmul"):
    M, K = x.shape
    N = w.shape[-1]
    tm, tn, tk = _tile(M, tm), _tile(N, tn), _tile(K, tk)
    nk = K // tk
    if layer is None:
        w_spec = pl.BlockSpec((tk, tn), lambda i, j, k: (k, j))
    else:
        w_spec = pl.BlockSpec((1, tk, tn), lambda i, j, k: (layer, k, j))
    in_specs = [pl.BlockSpec((tm, tk), lambda i, j, k: (i, k)), w_spec]
    blocks = [_nbytes((tm, tk), x.dtype), _nbytes((tk, tn), w.dtype)]
    arrays = []
    for arr, kind in extras:
        if kind == "col":
            in_specs.append(pl.BlockSpec((1, tn), lambda i, j, k: (0, j)))
            blocks.append(_nbytes((SUBLANES, tn), arr.dtype))
        elif kind == "row":
            in_specs.append(pl.BlockSpec((tm, arr.shape[1]), lambda i, j, k: (i, 0)))
            blocks.append(_nbytes((tm, arr.shape[1]), arr.dtype))
        else:
            in_specs.append(pl.BlockSpec(arr.shape, lambda i, j, k: (0,) * arr.ndim))
            blocks.append(_nbytes(arr.shape, arr.dtype))
        arrays.append(arr)
    out_specs, out_shape = [], []
    for n_tot, n_blk, dt in outs:
        n_blk = tn if n_blk is None else n_blk
        out_specs.append(pl.BlockSpec((tm, n_blk), lambda i, j, k: (i, j)))
        out_shape.append(jax.ShapeDtypeStruct((M, n_tot), dt))
        blocks.append(_nbytes((tm, n_blk), dt))
    scratch = [pltpu.VMEM((tm, tn), F32)] if nk > 1 else []
    return pl.pallas_call(
        functools.partial(_mm_kernel, nk=nk, n_extra=len(arrays), n_out=len(outs), epilogue=epilogue),
        grid=(M // tm, N // tn, nk),
        in_specs=in_specs,
        out_specs=out_specs,
        out_shape=out_shape,
        scratch_shapes=scratch,
        compiler_params=pltpu.CompilerParams(
            dimension_semantics=("arbitrary", "arbitrary", "arbitrary"),
            vmem_limit_bytes=_vmem_limit(blocks, 2 * _nbytes((tm, tn), F32))),
        name=name,
    )(x, w, *arrays)


def _mm_ws_kernel(x_ref, *rest, n_w, n_extra, n_out, epilogue):
    w_refs = rest[:n_w]
    extras = rest[n_w:n_w + n_extra]
    outs = rest[n_w + n_extra:n_w + n_extra + n_out]
    wbf = rest[n_w + n_extra + n_out:]

    @pl.when(pl.program_id(1) == 0)
    def _():
        for w_ref, wb in zip(w_refs, wbf):
            wb[...] = w_ref[0].astype(BF16)

    x = x_ref[...]
    prods = [jnp.dot(x, wb[...], preferred_element_type=F32) for wb in wbf]
    epilogue(prods[0] if n_w == 1 else prods, extras, outs)


def _mm_ws_call(x, ws, layer, col0, n_cols, epilogue, outs, *, extras=(), tm=1024, tn=512, name="matmul_ws"):
    M, K = x.shape
    tm, tn = _tile(M, tm), _tile(n_cols, tn)
    assert col0 % tn == 0
    off = col0 // tn
    in_specs = [pl.BlockSpec((tm, K), lambda j, i: (i, 0))]
    in_specs += [pl.BlockSpec((1, K, tn), lambda j, i: (layer, 0, j + off))] * len(ws)
    blocks = [_nbytes((tm, K), x.dtype)] + [_nbytes((K, tn), F32)] * len(ws)
    arrays = []
    for arr, kind in extras:
        if kind == "col":
            in_specs.append(pl.BlockSpec((1, tn), lambda j, i: (0, j)))
            blocks.append(_nbytes((SUBLANES, tn), arr.dtype))
        elif kind == "row":
            in_specs.append(pl.BlockSpec((tm, arr.shape[1]), lambda j, i: (i, 0)))
            blocks.append(_nbytes((tm, arr.shape[1]), arr.dtype))
        else:
            in_specs.append(pl.BlockSpec(arr.shape, lambda j, i: (0,) * arr.ndim))
            blocks.append(_nbytes(arr.shape, arr.dtype))
        arrays.append(arr)
    out_specs, out_shape = [], []
    for n_tot, n_blk, dt in outs:
        n_blk = tn if n_blk is None else n_blk
        out_specs.append(pl.BlockSpec((tm, n_blk), lambda j, i: (i, j)))
        out_shape.append(jax.ShapeDtypeStruct((M, n_tot), dt))
        blocks.append(_nbytes((tm, n_blk), dt))
    scratch_bytes = len(ws) * _nbytes((K, tn), BF16)
    return pl.pallas_call(
        functools.partial(_mm_ws_kernel, n_w=len(ws), n_extra=len(arrays), n_out=len(outs), epilogue=epilogue),
        grid=(n_cols // tn, M // tm),
        in_specs=in_specs,
        out_specs=out_specs,
        out_shape=out_shape,
        scratch_shapes=[pltpu.VMEM((K, tn), BF16)] * len(ws),
        compiler_params=pltpu.CompilerParams(
            dimension_semantics=("arbitrary", "arbitrary"),
            vmem_limit_bytes=_vmem_limit(blocks, scratch_bytes + (len(ws) + 1) * _nbytes((tm, tn), F32))),
        name=name,
    )(x, *ws, *arrays)


def _ep_swiglu(acc, extras, outs):
    g, u = acc
    outs[0][...] = (_silu(g) * u).astype(BF16)


def _ep_bf16(acc, extras, outs):
    outs[0][...] = acc.astype(BF16)


def _ep_silu(acc, extras, outs):
    outs[0][...] = _silu(acc).astype(BF16)


def _ep_forget(acc, extras, outs):
    lb = extras[0][...]
    f = lb + (1.0 - lb) * _sigmoid(acc)
    outs[0][...] = jnp.log2(f)
    outs[1][...] = (1.0 - f).astype(BF16)


def _ep_rms(acc, extras, outs):
    outs[0][...] = _rms(acc, extras[0][...]).astype(BF16)


def _ep_q_rope(acc, extras, outs):
    cos_t, sin_t = extras[0][...], extras[1][...]
    for h in range(acc.shape[1] // Q_HEAD_PAD):
        lo = h * Q_HEAD_PAD
        outs[0][:, lo:lo + LANES] = (acc[:, lo:lo + LANES] * Q_SCALE_LOG2).astype(BF16)
        pe = _rope_block(acc[:, lo + LANES:lo + Q_HEAD_PAD], cos_t, sin_t)
        outs[0][:, lo + LANES:lo + Q_HEAD_PAD] = (pe * Q_SCALE_LOG2).astype(BF16)


def _uq_pad_kernel(w_ref, o_ref):
    w = w_ref[0]
    z = jnp.zeros((w.shape[0], Q_HEAD_PAD - QK_HEAD_DIM), F32)
    o_ref[...] = jnp.concatenate([w[:, :QK_HEAD_DIM], z, w[:, QK_HEAD_DIM:], z], axis=1).astype(BF16)


def _uq_pad_call(w_uq, layer, heads):
    L, Q, N = w_uq.shape
    assert heads % 2 == 0 and N == heads * QK_HEAD_DIM
    return pl.pallas_call(
        _uq_pad_kernel,
        grid=(heads // 2,),
        in_specs=[pl.BlockSpec((1, Q, 2 * QK_HEAD_DIM), lambda p: (layer, 0, p))],
        out_specs=pl.BlockSpec((Q, 2 * Q_HEAD_PAD), lambda p: (0, p)),
        out_shape=jax.ShapeDtypeStruct((Q, heads * Q_HEAD_PAD), BF16),
        compiler_params=pltpu.CompilerParams(dimension_semantics=("arbitrary",)),
        name="uq_pad",
    )(w_uq)


def _ep_dkv(acc, extras, outs):
    g, cos_t, sin_t = extras[0][...], extras[1][...], extras[2][...]
    outs[0][...] = _rms(acc[:, :KV_LORA], g).astype(BF16)
    outs[1][...] = _rope_block(acc[:, KV_LORA:KV_LORA + LANES], cos_t, sin_t).astype(BF16)


def _boundary_cumsum(b, m):
    T, W = b.shape
    if 2 * m >= SUBLANES:
        nb = T // (2 * m)
        b3 = b.reshape(nb, 2 * m, W)
        return jnp.broadcast_to(b3[:, m - 1:m, :], (nb, 2 * m, W)).reshape(T, W)
    b3 = b.reshape(T // SUBLANES, SUBLANES, W)
    sub = lax.broadcasted_iota(jnp.int32, b3.shape, 1)
    r = jnp.broadcast_to(b3[:, m - 1:m, :], b3.shape)
    for blk in range(1, SUBLANES // (2 * m)):
        row = blk * 2 * m + m - 1
        r = jnp.where(sub >= blk * 2 * m, jnp.broadcast_to(b3[:, row:row + 1, :], b3.shape), r)
    return r.reshape(T, W)


def _hgrn_kernel(q_ref, k_ref, v_ref, lf_ref, gs_ref, ng_ref, tril_ref, sgn_ref, rowsel_ref, pmask_ref,
                 *rest, n_chunks, nh, side_blocks):
    T = HG_CHUNK

    n_side = len(side_blocks)
    side_in = rest[:n_side]
    o_ref = rest[n_side]
    side_out = rest[n_side + 1:2 * n_side + 1]
    st_ref = rest[2 * n_side + 1]
    step = pl.program_id(0) * pl.num_programs(1) + pl.program_id(1)
    for w_ref, c_ref, (nb_in, nb_out) in zip(side_in, side_out, side_blocks):
        _side_cast(step, w_ref, c_ref, nb_in, nb_out)

    @pl.when(pl.program_id(1) == 0)
    def _():
        st_ref[...] = jnp.zeros_like(st_ref)

    nt = (((1,), (1,)), ((), ()))
    tn = (((0,), (0,)), ((), ()))
    heads = range(nh)
    cols = [slice(hh * HG_HEAD_DIM, (hh + 1) * HG_HEAD_DIM) for hh in heads]

    def chunk(c, carry):
        rows = pl.ds(pl.multiple_of(c * T, T), T)
        lf = [lf_ref[rows, cs] for cs in cols]
        b = []
        for hh in heads:
            hi = lf[hh].astype(BF16)
            r1 = lf[hh] - hi.astype(F32)
            mid = r1.astype(BF16)
            lo = (r1 - mid.astype(F32)).astype(BF16)
            parts = jnp.dot(tril_ref[...], jnp.concatenate([hi, mid, lo], axis=1),
                            preferred_element_type=F32)
            b.append(parts[:, :LANES] + parts[:, LANES:2 * LANES] + parts[:, 2 * LANES:])
        qb = [q_ref[rows, cs] for cs in cols]
        kb = [k_ref[rows, cs] for cs in cols]
        vb = [v_ref[rows, cs] for cs in cols]
        q = [t.astype(F32) for t in qb]
        k = [t.astype(F32) for t in kb]
        st = [st_ref[hh] for hh in heads]
        o = [lax.dot_general((q[hh] * jnp.exp2(b[hh])).astype(BF16), st[hh].astype(BF16), nt,
                             preferred_element_type=F32) for hh in heads]
        a = [jnp.where(pmask_ref[HG_LEVELS] > 0.5,
                       lax.dot_general(qb[hh], kb[hh], nt, preferred_element_type=F32), 0.0)
             for hh in heads]
        for lvl in range(HG_LEVELS):
            second = rowsel_ref[lvl] > 0.5
            for hh in heads:
                if lvl == 0:
                    arg = lf[hh] * rowsel_ref[0]
                else:
                    arg = (b[hh] - _boundary_cumsum(b[hh], 1 << lvl)) * sgn_ref[lvl]
                x = (jnp.where(second, q[hh], k[hh]) * jnp.exp2(arg)).astype(BF16)
                p = lax.dot_general(x, x, nt, preferred_element_type=F32)
                a[hh] = jnp.where(pmask_ref[lvl] > 0.5, p, a[hh])
        for hh in heads:
            o[hh] = o[hh] + jnp.dot(a[hh].astype(BF16), vb[hh], preferred_element_type=F32)
        for hh in heads:
            b_last = b[hh][T - 1:T, :]
            kt = (k[hh] * jnp.exp2(b_last - b[hh])).astype(BF16)
            st_ref[hh] = st[hh] * jnp.exp2(b_last) + lax.dot_general(vb[hh], kt, tn,
                                                                     preferred_element_type=F32)
        for hh in heads:
            y = _rms(o[hh], ng_ref[...]) * gs_ref[rows, cols[hh]].astype(F32)
            o_ref[rows, cols[hh]] = y.astype(o_ref.dtype)
        return carry

    lax.fori_loop(0, n_chunks, chunk, 0)


def _hgrn_masks():
    T = HG_CHUNK
    t = jnp.arange(T)
    tril = (t[:, None] >= t[None, :]).astype(BF16)
    rowsel, pmask = [], []
    for lvl in range(HG_LEVELS):
        m = 1 << lvl
        second = (t // m) % 2 == 1
        rowsel.append(jnp.broadcast_to(second[:, None], (T, HG_HEAD_DIM)))
        same = (t[:, None] // (2 * m)) == (t[None, :] // (2 * m))
        pmask.append(same & second[:, None] & (~second)[None, :])
    pmask.append(t[:, None] == t[None, :])
    rowsel = jnp.stack(rowsel).astype(F32)
    return tril, 2.0 * rowsel - 1.0, rowsel, jnp.stack(pmask).astype(F32)


def _hgrn_call(q, k, v, lf2, gs, norm_g, side=(), *, t_blk=512, nh=8):
    S, D = q.shape
    H = D // HG_HEAD_DIM
    t_blk = _tile(S, t_blk)
    nh = math.gcd(nh, H)
    assert t_blk % HG_CHUNK == 0
    tril, sgn, rowsel, pmask = _hgrn_masks()
    w_blk = nh * HG_HEAD_DIM
    n_s = S // t_blk
    seq = pl.BlockSpec((t_blk, w_blk), lambda h, s: (s, h))
    plans = [_side_plan(w, layer, (H // nh) * n_s, rows_out, cols_out, lambda h, s: h * n_s + s)
             for w, layer, rows_out, cols_out in side]

    def const(a):
        return pl.BlockSpec(a.shape, lambda h, s: (0,) * a.ndim)

    ng = norm_g.reshape(1, HG_HEAD_DIM)
    res = pl.pallas_call(
        functools.partial(_hgrn_kernel, n_chunks=t_blk // HG_CHUNK, nh=nh,
                          side_blocks=tuple(p[4] for p in plans)),
        grid=(H // nh, n_s),
        in_specs=[seq, seq, seq, seq, seq, const(ng), const(tril), const(sgn), const(rowsel), const(pmask)]
        + [p[1] for p in plans],
        out_specs=[seq] + [p[2] for p in plans],
        out_shape=[jax.ShapeDtypeStruct((S, D), BF16)] + [p[3] for p in plans],
        scratch_shapes=[pltpu.VMEM((nh, HG_HEAD_DIM, HG_HEAD_DIM), F32)],
        compiler_params=pltpu.CompilerParams(
            dimension_semantics=("arbitrary", "arbitrary"),
            vmem_limit_bytes=_vmem_limit([5 * _nbytes((t_blk, w_blk), BF16), _nbytes((t_blk, w_blk), F32)]
                                         + [p[5] for p in plans],
                                         64 * nh * _nbytes((HG_CHUNK, HG_HEAD_DIM), F32))),
        name="hgrn2_recurrence",
    )(q, k, v, lf2, gs, ng, tril, sgn, rowsel, pmask, *[p[0] for p in plans])
    return res[0], list(res[1:])


def _side_cast(step, w_ref, o_ref, nb_in, nb_out):
    val = w_ref[0]
    extra = o_ref.shape[1] - val.shape[1]
    if extra:
        val = jnp.concatenate([val, jnp.zeros((val.shape[0], extra), F32)], axis=1)
    val = val.astype(BF16)
    if nb_out != nb_in:
        val = jnp.where(jnp.minimum(step, nb_out - 1) < nb_in, val, jnp.zeros_like(val))
    o_ref[...] = val


def _side_plan(w, layer, steps, rows_out, cols_out, index_of_step):
    L, K, N = w.shape
    rb = next(r for r in range(16, rows_out + 1, 16)
              if K % r == 0 and rows_out % r == 0 and rows_out // r <= steps)
    nb_in, nb_out = K // rb, rows_out // rb
    in_spec = pl.BlockSpec((1, rb, N), lambda *g: (layer, jnp.minimum(index_of_step(*g), nb_in - 1), 0))
    out_spec = pl.BlockSpec((rb, cols_out), lambda *g: (jnp.minimum(index_of_step(*g), nb_out - 1), 0))
    out_shape = jax.ShapeDtypeStruct((rows_out, cols_out), BF16)
    nbytes = _nbytes((rb, N), F32) + _nbytes((rb, cols_out), BF16)
    return w, in_spec, out_spec, out_shape, (nb_in, nb_out), nbytes


def _attn_kernel(q_ref, kn_ref, v_ref, kpe_ref, *rest, tq, tk, tk_diag, nsub, side_blocks):
    n_side = len(side_blocks)
    side_in = rest[:n_side]
    o_ref = rest[n_side]
    side_out = rest[n_side + 1:2 * n_side + 1]
    m_ref, acc_ref = rest[2 * n_side + 1:]
    qi = pl.program_id(1)
    step = pl.program_id(0) * pl.num_programs(1) + qi
    tb = tq * nsub
    kv_per_block = tb // tk
    nt = (((1,), (1,)), ((), ()))
    m_ref[...] = jnp.full_like(m_ref, -jnp.inf)
    acc_ref[...] = jnp.zeros_like(acc_ref)

    def tile_update(a, kc, ve, mask_off):
        q = q_ref[a * tq:(a + 1) * tq, :]
        s = lax.dot_general(q, kc, nt, preferred_element_type=F32)
        if mask_off is not None:
            r = lax.broadcasted_iota(jnp.int32, s.shape, 0)
            c = lax.broadcasted_iota(jnp.int32, s.shape, 1)
            s = jnp.where(r + mask_off >= c, s, -jnp.inf)
        m_old = m_ref[a]
        m_new = jnp.maximum(m_old, s.max(axis=-1, keepdims=True))
        alpha = jnp.exp2(m_old - m_new)
        p = jnp.concatenate([jnp.exp2(s[:, cb * LANES:(cb + 1) * LANES] - m_new)
                             for cb in range(s.shape[1] // LANES)], axis=1).astype(BF16)
        pv = jnp.dot(p, ve, preferred_element_type=F32)
        acc_ref[a] = jnp.concatenate([alpha, alpha], axis=1) * acc_ref[a] + pv
        m_ref[a] = m_new

    def load_kv(start, size):
        rows = pl.ds(pl.multiple_of(start, size), size)
        kc = jnp.concatenate([kn_ref[rows, :], kpe_ref[rows, :]], axis=1)
        ve = jnp.concatenate([v_ref[rows, :], jnp.ones((size, LANES), BF16)], axis=1)
        return kc, ve

    def body(j, carry):
        kc, ve = load_kv(j * tk, tk)
        for a in range(nsub):
            tile_update(a, kc, ve, None)
        return carry

    lax.fori_loop(0, qi * kv_per_block, body, 0)
    for b in range(tb // tk_diag):
        kc, ve = load_kv(qi * tb + b * tk_diag, tk_diag)
        for a in range(nsub):
            if b * tk_diag > (a + 1) * tq - 1:
                continue
            unmasked = (b + 1) * tk_diag - 1 <= a * tq
            tile_update(a, kc, ve, None if unmasked else a * tq - b * tk_diag)
    for w_ref, c_ref, (nb_in, nb_out) in zip(side_in, side_out, side_blocks):
        _side_cast(step, w_ref, c_ref, nb_in, nb_out)
    for a in range(nsub):
        acc = acc_ref[a]
        o_ref[a * tq:(a + 1) * tq, :] = (acc[:, :V_HEAD_DIM] / acc[:, V_HEAD_DIM:]).astype(o_ref.dtype)


def _attn_call(q, kv, kpe, side=(), *, tq=256, tk=2048, tk_diag=256, nsub=8):
    S = q.shape[0]
    H = q.shape[1] // Q_HEAD_PAD
    nsub = min(nsub, S // tq)
    tb = tq * nsub
    tk = _tile(tb, tk)
    assert S % tb == 0 and tb % tk == 0 and tb % tk_diag == 0
    n_i = S // tb
    plans = [_side_plan(w, layer, H * n_i, rows_out, cols_out, lambda h, i: h * n_i + i)
             for w, layer, rows_out, cols_out in side]
    out_row = pl.BlockSpec((tb, V_HEAD_DIM), lambda h, i: (i, h))
    res = pl.pallas_call(
        functools.partial(_attn_kernel, tq=tq, tk=tk, tk_diag=tk_diag, nsub=nsub,
                          side_blocks=tuple(p[4] for p in plans)),
        grid=(H, n_i),
        in_specs=[
            pl.BlockSpec((tb, Q_HEAD_PAD), lambda h, i: (i, h)),
            pl.BlockSpec((S, NOPE_DIM), lambda h, i: (0, 2 * h)),
            pl.BlockSpec((S, V_HEAD_DIM), lambda h, i: (0, 2 * h + 1)),
            pl.BlockSpec((S, LANES), lambda h, i: (0, 0)),
        ] + [p[1] for p in plans],
        out_specs=[out_row] + [p[2] for p in plans],
        out_shape=[jax.ShapeDtypeStruct((S, H * V_HEAD_DIM), BF16)] + [p[3] for p in plans],
        scratch_shapes=[pltpu.VMEM((nsub, tq, LANES), F32), pltpu.VMEM((nsub, tq, 2 * LANES), F32)],
        compiler_params=pltpu.CompilerParams(
            dimension_semantics=("arbitrary", "arbitrary"),
            vmem_limit_bytes=_vmem_limit([3 * _nbytes((S, LANES), BF16), _nbytes((tb, Q_HEAD_PAD), BF16),
                                          _nbytes((tb, V_HEAD_DIM), BF16)] + [p[5] for p in plans],
                                         2 * nsub * _nbytes((tq, tk), F32))),
        name="mla_attention",
    )(q, kv, kv, kpe, *[p[0] for p in plans])
    return res[0], list(res[1:])


def _ffn_up_kernel(x_ref, wg_ref, wu_ref, o_ref):
    x = x_ref[...]
    g = jnp.dot(x, wg_ref[...], preferred_element_type=F32)
    u = jnp.dot(x, wu_ref[...], preferred_element_type=F32)
    o_ref[...] = (_silu(g) * u).astype(o_ref.dtype)


def _ffn_up_call(h, wg, wu, *, tm=1024, tn=512):
    M, K = h.shape
    N = wg.shape[1]
    tm, tn = _tile(M, tm), _tile(N, tn)
    wspec = pl.BlockSpec((K, tn), lambda i, j: (0, j))
    return pl.pallas_call(
        _ffn_up_kernel,
        grid=(M // tm, N // tn),
        in_specs=[pl.BlockSpec((tm, K), lambda i, j: (i, 0)), wspec, wspec],
        out_specs=pl.BlockSpec((tm, tn), lambda i, j: (i, j)),
        out_shape=jax.ShapeDtypeStruct((M, N), BF16),
        compiler_params=pltpu.CompilerParams(
            dimension_semantics=("arbitrary", "arbitrary"),
            vmem_limit_bytes=_vmem_limit([_nbytes((tm, K), BF16), 2 * _nbytes((K, tn), BF16),
                                          _nbytes((tm, tn), BF16)], 4 * _nbytes((tm, tn), F32))),
        name="ffn_up",
    )(h, wg, wu)


def _pad_cols(w, n):
    return jnp.pad(w, ((0, 0), (0, n - w.shape[1])))


FFN_PAD = 512


def _swiglu(h, layer, w_gate, w_up, w_down, hosted=None, down_bf16=None):
    F, D = w_gate.shape[2], w_gate.shape[1]
    if hosted is None:
        (a,) = _mm_ws_call(h, [w_gate, w_up], layer, 0, F, _ep_swiglu, [(F, None, BF16)], tn=256, name="ffn_up")
        wd = w_down[layer].astype(BF16) if down_bf16 is None else down_bf16
        (y,) = _mm_call(a, wd, _ep_bf16, [(D, None, BF16)],
                        tk=F // 2 if F % (2 * LANES) == 0 else F, name="ffn_down")
        return y
    wg, wu, wd = hosted
    Fp = wd.shape[0]
    a = _ffn_up_call(h, wg, wu, tn=FFN_PAD)
    (y,) = _mm_call(a, wd, _ep_bf16, [(D, None, BF16)], tn=1024,
                    tk=Fp // 4 if Fp % (4 * 2 * LANES) == 0 else Fp, name="ffn_down")
    return y


def kernel(x, c, positions, ada_w, ada_b, norm_g, ffn_w_gate, ffn_w_up, ffn_w_down, hg_w_in, hg_lb_logits, hg_norm_g, hg_w_out, mla_w_dq, mla_q_norm_g, mla_w_uq, mla_w_o, kv_norm_in_g, kv_w_dkv, kv_norm_g, kv_w_ukv):
    B, S, D = x.shape
    assert B == 1
    depth = ada_w.shape[0]
    n_a = hg_w_in.shape[0]
    mla_heads = kv_w_ukv.shape[1] // (NOPE_DIM + V_HEAD_DIM)
    fdim = hg_lb_logits.shape[1]

    xs = x.reshape(S, D)
    mod = _ada_call(c, ada_w, ada_b).reshape(depth, 6, D)
    cos_t, sin_t = _rope_call(positions)
    lower_bounds = jnp.cumsum(jax.nn.softmax(hg_lb_logits.astype(F32), axis=0), axis=0)

    ffn_dim = ffn_w_gate.shape[2]
    ffn_pad = -(-ffn_dim // FFN_PAD) * FFN_PAD

    h = _prenorm_call(xs, norm_g[0, 0], mod[0, 1], mod[0, 0])
    kv = kpe = None
    n_dkv = KV_LORA + LANES
    w_dkv_bf = w_ukv_bf = w_dq_bf = None
    for l in range(depth):
        down_bf16 = None
        sh_m, sc_m, g_m, sh_f, sc_f, g_f = (mod[l, i] for i in range(6))
        hosted = None
        if l < n_a:
            lb = lower_bounds[l].reshape(1, fdim)
            w_in = [hg_w_in]
            (q,) = _mm_ws_call(h, w_in, l, 0, fdim, _ep_silu, [(fdim, None, BF16)], name="hg_q")
            lf, k = _mm_ws_call(h, w_in, l, fdim, fdim, _ep_forget, [(fdim, None, F32), (fdim, None, BF16)],
                                extras=[(lb, "col")], name="hg_f")
            (v,) = _mm_ws_call(h, w_in, l, 2 * fdim, D, _ep_bf16, [(D, None, BF16)], name="hg_i")
            (gs,) = _mm_ws_call(h, w_in, l, 2 * fdim + D, D, _ep_silu, [(D, None, BF16)], name="hg_g")
            side = [(ffn_w_down, l, ffn_pad, D), (ffn_w_gate, l, D, ffn_pad), (ffn_w_up, l, D, ffn_pad)]
            host_mla = l == n_a - 1 and depth > n_a
            if host_mla:
                side += [(kv_w_dkv[None], 0, D, n_dkv), (kv_w_ukv[None], 0, KV_LORA, kv_w_ukv.shape[1]),
                         (mla_w_dq, 0, D, mla_w_dq.shape[2])]
            o, casts = _hgrn_call(q, k, v, lf, gs, hg_norm_g[l], side)
            hosted = (casts[1], casts[2], casts[0])
            if host_mla:
                w_dkv_bf, w_ukv_bf, w_dq_bf = casts[3:]
            (y,) = _mm_ws_call(o, [hg_w_out], l, 0, D, _ep_bf16, [(D, None, BF16)], name="hg_out")
        else:
            j = l - n_a
            q_lora = mla_w_dq.shape[2]
            w_dq = w_dq_bf if (j == 0 and w_dq_bf is not None) else mla_w_dq[j].astype(BF16)
            (qd,) = _mm_call(h, w_dq, _ep_rms, [(q_lora, q_lora, BF16)],
                             extras=[(mla_q_norm_g[j].reshape(1, q_lora), "col")], tn=q_lora, name="mla_dq")
            w_uq = _uq_pad_call(mla_w_uq, j, mla_heads)
            (q,) = _mm_call(qd, w_uq, _ep_q_rope, [(w_uq.shape[1], None, BF16)],
                            extras=[(cos_t, "row"), (sin_t, "row")], tn=2048, name="mla_uq")
            o, (wg, wu, wd, wo) = _attn_call(q, kv, kpe, side=[
                (ffn_w_gate, l, D, ffn_pad), (ffn_w_up, l, D, ffn_pad),
                (ffn_w_down, l, ffn_pad, D), (mla_w_o, j, mla_w_o.shape[1], D)])
            hosted = (wg, wu, wd)
            (y,) = _mm_call(o, wo, _ep_bf16, [(D, None, BF16)], name="mla_o")
        xs, h = _resid_call(xs, y, g_m, norm_g[l, 1], [(norm_g[l, 2], sc_f, sh_f)])
        y = _swiglu(h, l, ffn_w_gate, ffn_w_up, ffn_w_down, hosted, down_bf16)
        norms = []
        if l + 1 < depth:
            norms.append((norm_g[l + 1, 0], mod[l + 1, 1], mod[l + 1, 0]))
        if l == n_a - 1:
            norms.append((kv_norm_in_g,))
        res = _resid_call(xs, y, g_f, norm_g[l, 3], norms)
        xs = res[0]
        if l + 1 < depth:
            h = res[1]
        if l == n_a - 1:
            src = res[-1]
            if w_dkv_bf is None:
                w_dkv_bf, w_ukv_bf = _pad_cols(kv_w_dkv, n_dkv).astype(BF16), kv_w_ukv.astype(BF16)
            ckvn, kpe = _mm_call(src, w_dkv_bf, _ep_dkv,
                                 [(KV_LORA, KV_LORA, BF16), (LANES, LANES, BF16)],
                                 extras=[(kv_norm_g.reshape(1, KV_LORA), "full"), (cos_t, "row"), (sin_t, "row")],
                                 tn=n_dkv, name="mla_dkv")
            (kv,) = _mm_call(ckvn, w_ukv_bf, _ep_bf16, [(kv_w_ukv.shape[1], None, BF16)],
                             tm=2048, tn=1024, name="mla_ukv")
    return xs.reshape(B, S, D)
```
